```python
import jax
import jax.numpy as jnp
from jax import lax
import numpy as np

D_MODEL = 1024
BATCH = 8
SEQ = 4096
DEPTH = 4

CTX_LEN = 256
GRID_W = 64
D_MIX = D_MODEL
D_CONV = D_MIX // 2
CONV_WIDTH = 31
H_MLSTM = 4
D_MLSTM = D_MIX - D_CONV
HD_MLSTM = D_MLSTM // H_MLSTM
MLSTM_CHUNK = 64
N_GATES = 4
EVEN_IN = 2 * D_CONV + 4 * D_MLSTM + N_GATES * H_MLSTM
D_POOL = D_MIX // 4
POOL_WINDOWS = (2, 4, 8, 16)
POOL_GROUP = D_POOL // len(POOL_WINDOWS)
HEAD_DIM = 128
N_Q_HEADS = (D_MIX - D_POOL) // HEAD_DIM
N_KV_HEADS = 2
Q_PER_KV = N_Q_HEADS // N_KV_HEADS
D_Q = N_Q_HEADS * HEAD_DIM
D_KV = N_KV_HEADS * HEAD_DIM
ODD_IN = D_POOL + D_Q + 2 * D_KV
Q_BLOCK = 128
ROPE_THETA = 10000.0
D_FF = 256 * ((8 * D_MODEL // 3 + 255) // 256)
FFN_CONV_WIDTH = 3
N_EVEN = (DEPTH + 1) // 2
N_ODD = DEPTH // 2
EPS = 1e-6
ADA_STD = 0.5

kernel_name = 'hybrid_conv_mlstm_pool_gqa_dit_block'


def rms_norm(x, g):
    xf = x.astype(jnp.float32)
    y = xf * lax.rsqrt(jnp.mean(xf * xf, axis=-1, keepdims=True) + EPS)
    return (y * g.astype(jnp.float32)).astype(x.dtype)


def layer_norm(x, g, b):
    xf = x.astype(jnp.float32)
    mu = jnp.mean(xf, axis=-1, keepdims=True)
    var = jnp.mean(jnp.square(xf - mu), axis=-1, keepdims=True)
    y = (xf - mu) * lax.rsqrt(var + EPS) * g.astype(jnp.float32) + b.astype(jnp.float32)
    return y.astype(x.dtype)


def modulate(h, shift, scale):
    return h * (1 + scale) + shift


def dwconv(x, w):
    return lax.conv_general_dilated(x, w[:, None, :].astype(x.dtype), window_strides=(1,), padding='SAME',
                                    dimension_numbers=('NWC', 'WIO', 'NWC'), feature_group_count=x.shape[-1])


def axial_rope_angles(n_tokens):
    rows = n_tokens // GRID_W
    row = jnp.repeat(jnp.arange(rows), GRID_W).astype(jnp.float32)
    col = jnp.tile(jnp.arange(GRID_W), rows).astype(jnp.float32)
    n_freq = HEAD_DIM // 4
    inv = ROPE_THETA ** (-jnp.arange(n_freq, dtype=jnp.float32) / n_freq)
    ang = jnp.concatenate([row[:, None] * inv, col[:, None] * inv], axis=-1)
    return jnp.cos(ang), jnp.sin(ang)


def apply_rope(x, cos, sin):
    xf = x.astype(jnp.float32).reshape(*x.shape[:-1], HEAD_DIM // 2, 2)
    xe, xo = xf[..., 0], xf[..., 1]
    cs, sn = cos[:, None, :], sin[:, None, :]
    out = jnp.stack([xe * cs - xo * sn, xe * sn + xo * cs], axis=-1).reshape(x.shape)
    return out.astype(x.dtype)


def conv_module(glu_in, dw_w, ln_g, ln_b):
    a, b = jnp.split(glu_in, 2, axis=-1)
    u = a * jax.nn.sigmoid(b)
    u = dwconv(u, dw_w)
    u = layer_norm(u, ln_g, ln_b)
    return jax.nn.silu(u)


def mlstm_chunkwise(q, k, v, i_pre, log_f, state):
    B_, H_, T, d = q.shape
    nc = T // MLSTM_CHUNK

    def chunks(a):
        return jnp.moveaxis(a.reshape(B_, H_, nc, MLSTM_CHUNK, *a.shape[3:]), 2, 0)

    lower = jnp.tril(jnp.ones((MLSTM_CHUNK, MLSTM_CHUNK), dtype=bool))

    def step(carry, inp):
        C, n, m = carry
        qj, kj, vj, ij, fj = inp
        b = jnp.cumsum(fj, axis=-1)
        dmat = jnp.where(lower, b[..., :, None] - b[..., None, :] + ij[..., None, :], -jnp.inf)
        m_inter = b + m[..., None]
        m_t = jnp.maximum(m_inter, jnp.max(dmat, axis=-1))
        w_inter = jnp.exp(m_inter - m_t)
        s = jnp.einsum('bhld,bhsd->bhls', qj, kj) * jnp.exp(dmat - m_t[..., None])
        num = w_inter[..., None] * jnp.einsum('bhld,bhde->bhle', qj, C) + jnp.einsum('bhls,bhse->bhle', s, vj)
        den = w_inter * jnp.einsum('bhld,bhd->bhl', qj, n) + jnp.sum(s, axis=-1)
        h = num / jnp.maximum(jnp.abs(den), jnp.exp(-m_t))[..., None]
        b_last = b[..., -1]
        src = b_last[..., None] - b + ij
        m_new = jnp.maximum(b_last + m, jnp.max(src, axis=-1))
        a = jnp.exp(b_last + m - m_new)
        e = jnp.exp(src - m_new[..., None])
        C_new = a[..., None, None] * C + jnp.einsum('bhs,bhsd,bhse->bhde', e, kj, vj)
        n_new = a[..., None] * n + jnp.einsum('bhs,bhsd->bhd', e, kj)
        return (C_new, n_new, m_new), h

    state, hs = lax.scan(step, state, (chunks(q), chunks(k), chunks(v), chunks(i_pre), chunks(log_f)))
    return jnp.moveaxis(hs, 0, 2).reshape(B_, H_, T, d), state


def mlstm_inputs(p, gate_b):
    B_, T = p.shape[:2]
    off = 2 * D_CONV

    def heads(a):
        return a.reshape(B_, T, H_MLSTM, HD_MLSTM).transpose(0, 2, 1, 3).astype(jnp.float32)

    q = heads(p[..., off:off + D_MLSTM])
    k = heads(p[..., off + D_MLSTM:off + 2 * D_MLSTM]) * (HD_MLSTM ** -0.5)
    v = heads(p[..., off + 2 * D_MLSTM:off + 3 * D_MLSTM])
    o = p[..., off + 3 * D_MLSTM:off + 4 * D_MLSTM]
    g = p[..., off + 4 * D_MLSTM:].astype(jnp.float32).reshape(B_, T, N_GATES, H_MLSTM) + gate_b.astype(jnp.float32)
    g = g.transpose(2, 0, 3, 1)
    return q, k, v, o, g


def bidir_mlstm(qc, kc, vc, gc, q, k, v, g):
    B_ = q.shape[0]
    zero = (jnp.zeros((B_, H_MLSTM, HD_MLSTM, HD_MLSTM), jnp.float32),
            jnp.zeros((B_, H_MLSTM, HD_MLSTM), jnp.float32),
            jnp.zeros((B_, H_MLSTM), jnp.float32))
    rev = lambda a: jnp.flip(a, axis=2)
    lsig = jax.nn.log_sigmoid
    hc_f, st_f = mlstm_chunkwise(qc, kc, vc, gc[0], lsig(gc[1]), zero)
    hc_b, st_b = mlstm_chunkwise(rev(qc), rev(kc), rev(vc), rev(gc[2]), rev(lsig(gc[3])), zero)
    h_f, _ = mlstm_chunkwise(q, k, v, g[0], lsig(g[1]), st_f)
    h_b, _ = mlstm_chunkwise(rev(q), rev(k), rev(v), rev(g[2]), rev(lsig(g[3])), st_b)
    return h_f + rev(h_b), hc_f + rev(hc_b)


def mlstm_merge(h, o, norm_g):
    B_, _, T, _ = h.shape
    hn = rms_norm(h.transpose(0, 2, 1, 3), norm_g.reshape(H_MLSTM, HD_MLSTM))
    return hn.reshape(B_, T, D_MLSTM).astype(o.dtype) * jax.nn.sigmoid(o)


def even_mixer(h, hc, w_in, w_out, dw_w, ln_g, ln_b, gate_b, norm_g, need_ctx):
    p = h @ w_in
    pc = hc @ w_in
    q, k, v, o, g = mlstm_inputs(p, gate_b)
    qc, kc, vc, oc, gc = mlstm_inputs(pc, gate_b)
    h_m, hc_m = bidir_mlstm(qc, kc, vc, gc, q, k, v, g)
    y = jnp.concatenate([conv_module(p[..., :2 * D_CONV], dw_w, ln_g, ln_b),
                         mlstm_merge(h_m, o, norm_g).astype(h.dtype)], axis=-1) @ w_out
    yc = None
    if need_ctx:
        yc = jnp.concatenate([conv_module(pc[..., :2 * D_CONV], dw_w, ln_g, ln_b),
                              mlstm_merge(hc_m, oc, norm_g).astype(hc.dtype)], axis=-1) @ w_out
    return y, yc


def multiscale_pool(u, pool_w, pool_scale):
    B_, T, _ = u.shape
    uf = u.astype(jnp.float32)
    csum = jnp.concatenate([jnp.zeros_like(uf[:, :1]), jnp.cumsum(uf, axis=1)], axis=1)
    t = jnp.arange(T)
    outs = []
    for gi, w in enumerate(POOL_WINDOWS):
        lo = jnp.clip(t - w // 2, 0, T)
        hi = jnp.clip(t + w // 2, 0, T)
        sl = slice(gi * POOL_GROUP, (gi + 1) * POOL_GROUP)
        cs = csum[..., sl]
        mean = (cs[:, hi] - cs[:, lo]) / (hi - lo).astype(jnp.float32)[:, None]
        outs.append(mean - uf[..., sl])
    dpool = jnp.stack(outs, axis=2)
    y = jnp.einsum('btgc,gce->btge', dpool, pool_w.astype(jnp.float32)).reshape(B_, T, D_POOL)
    return (y * pool_scale.astype(jnp.float32)).astype(u.dtype)


def latent_attention(q, k, v, kc, vc):
    B_, Hk, G, T, d = q.shape
    k_all = jnp.concatenate([kc, k], axis=2)
    v_all = jnp.concatenate([vc, v], axis=2)
    nb = T // Q_BLOCK
    qb = jnp.moveaxis(q.reshape(B_, Hk, G, nb, Q_BLOCK, d), 3, 0)
    scale = HEAD_DIM ** -0.5

    def block(qblk):
        s = jnp.einsum('bhgqd,bhkd->bhgqk', qblk, k_all).astype(jnp.float32) * scale
        pr = jax.nn.softmax(s, axis=-1).astype(v_all.dtype)
        return jnp.einsum('bhgqk,bhkd->bhgqd', pr, v_all)

    o = lax.map(block, qb)
    return jnp.moveaxis(o, 0, 3).reshape(B_, Hk, G, T, d)


def context_attention(q, k, v):
    s = jnp.einsum('bhgqd,bhkd->bhgqk', q, k).astype(jnp.float32) * (HEAD_DIM ** -0.5)
    return jnp.einsum('bhgqk,bhkd->bhgqd', jax.nn.softmax(s, axis=-1).astype(v.dtype), v)


def attn_inputs(p, q_g, k_g):
    B_, T = p.shape[:2]
    q = p[..., D_POOL:D_POOL + D_Q].reshape(B_, T, N_Q_HEADS, HEAD_DIM)
    k = p[..., D_POOL + D_Q:D_POOL + D_Q + D_KV].reshape(B_, T, N_KV_HEADS, HEAD_DIM)
    v = p[..., D_POOL + D_Q + D_KV:].reshape(B_, T, N_KV_HEADS, HEAD_DIM)
    return rms_norm(q, q_g), rms_norm(k, k_g), v


def to_groups(q):
    B_, T = q.shape[:2]
    return q.reshape(B_, T, N_KV_HEADS, Q_PER_KV, HEAD_DIM).transpose(0, 2, 3, 1, 4)


def from_groups(o):
    B_, _, _, T, _ = o.shape
    return o.transpose(0, 3, 1, 2, 4).reshape(B_, T, D_Q)


def odd_mixer(h, hc, w_in, w_out, pool_w, pool_scale, q_g, k_g, cos, sin, need_ctx):
    p = h @ w_in
    pc = hc @ w_in
    q, k, v = attn_inputs(p, q_g, k_g)
    qc, kc, vc = attn_inputs(pc, q_g, k_g)
    q = apply_rope(q, cos, sin)
    k = apply_rope(k, cos, sin)
    kv = lambda a: a.transpose(0, 2, 1, 3)
    attn = from_groups(latent_attention(to_groups(q), kv(k), kv(v), kv(kc), kv(vc)))
    y = jnp.concatenate([multiscale_pool(p[..., :D_POOL], pool_w, pool_scale), attn], axis=-1) @ w_out
    yc = None
    if need_ctx:
        attn_c = from_groups(context_attention(to_groups(qc), kv(kc), kv(vc)))
        yc = jnp.concatenate([multiscale_pool(pc[..., :D_POOL], pool_w, pool_scale), attn_c], axis=-1) @ w_out
    return y, yc


def conv_ffn(h, w_in, conv_w, w_out):
    g, v = jnp.split(h @ w_in, 2, axis=-1)
    g = dwconv(g, conv_w)
    return (jax.nn.silu(g) * v) @ w_out


def setup_inputs(seed: int = 0) -> dict:
    key = jax.random.key(seed)
    ks = iter(jax.random.split(key, 40))
    f32 = jnp.float32

    def nrm(shape, std):
        return jax.random.normal(next(ks), shape, f32) * std

    def gain(shape):
        return 1.0 + 0.05 * jax.random.normal(next(ks), shape, f32)

    f_bias = jnp.linspace(3.0, 6.0, H_MLSTM, dtype=f32)
    gate_b = jnp.stack([nrm((N_EVEN, H_MLSTM), 0.1),
                        f_bias + nrm((N_EVEN, H_MLSTM), 0.1),
                        nrm((N_EVEN, H_MLSTM), 0.1),
                        f_bias + nrm((N_EVEN, H_MLSTM), 0.1)], axis=1)
    return {
        'x': nrm((BATCH, SEQ, D_MODEL), 1.0),
        'c': nrm((BATCH, D_MODEL), 1.0),
        'ctx': nrm((BATCH, CTX_LEN, D_MODEL), 1.0),
        'c_ctx': nrm((D_MODEL,), 1.0),
        'ada_w': nrm((DEPTH, D_MODEL, 6 * D_MODEL), ADA_STD * D_MODEL ** -0.5),
        'ada_b': nrm((DEPTH, 6 * D_MODEL), 0.02),
        'mix_pre_g': gain((DEPTH, D_MODEL)),
        'mix_post_g': gain((DEPTH, D_MODEL)),
        'ffn_pre_g': gain((DEPTH, D_MODEL)),
        'ffn_post_g': gain((DEPTH, D_MODEL)),
        'ffn_w_in': nrm((DEPTH, D_MODEL, 2 * D_FF), D_MODEL ** -0.5),
        'ffn_conv_w': nrm((DEPTH, FFN_CONV_WIDTH, D_FF), FFN_CONV_WIDTH ** -0.5),
        'ffn_w_out': nrm((DEPTH, D_FF, D_MODEL), D_FF ** -0.5),
        'even_w_in': nrm((N_EVEN, D_MODEL, EVEN_IN), D_MODEL ** -0.5),
        'even_w_out': nrm((N_EVEN, D_MIX, D_MODEL), D_MIX ** -0.5),
        'conv_dw_w': nrm((N_EVEN, CONV_WIDTH, D_CONV), CONV_WIDTH ** -0.5),
        'conv_ln_g': gain((N_EVEN, D_CONV)),
        'conv_ln_b': nrm((N_EVEN, D_CONV), 0.02),
        'mlstm_gate_b': gate_b,
        'mlstm_norm_g': gain((N_EVEN, D_MLSTM)),
        'odd_w_in': nrm((N_ODD, D_MODEL, ODD_IN), D_MODEL ** -0.5),
        'odd_w_out': nrm((N_ODD, D_MIX, D_MODEL), D_MIX ** -0.5),
        'pool_w': nrm((N_ODD, len(POOL_WINDOWS), POOL_GROUP, POOL_GROUP), POOL_GROUP ** -0.5),
        'pool_scale': gain((N_ODD, D_POOL)),
        'q_norm_g': gain((N_ODD, HEAD_DIM)),
        'k_norm_g': gain((N_ODD, HEAD_DIM)),
    }


def reference(x, c, ctx, c_ctx, ada_w, ada_b, mix_pre_g, mix_post_g, ffn_pre_g, ffn_post_g,
              ffn_w_in, ffn_conv_w, ffn_w_out, even_w_in, even_w_out, conv_dw_w, conv_ln_g, conv_ln_b,
              mlstm_gate_b, mlstm_norm_g, odd_w_in, odd_w_out, pool_w, pool_scale, q_norm_g, k_norm_g):
    cos, sin = axial_rope_angles(x.shape[1])
    silu_c = jax.nn.silu(c)
    silu_cc = jax.nn.silu(c_ctx)
    xc = ctx
    for l in range(DEPTH):
        need_ctx = l < DEPTH - 1
        mod = jnp.split((silu_c @ ada_w[l] + ada_b[l])[:, None, :], 6, axis=-1)
        mod_c = jnp.split(silu_cc @ ada_w[l] + ada_b[l], 6, axis=-1)
        h = modulate(rms_norm(x, mix_pre_g[l]), mod[0], mod[1])
        hc = modulate(rms_norm(xc, mix_pre_g[l]), mod_c[0], mod_c[1])
        if l % 2 == 0:
            e = l // 2
            y, yc = even_mixer(h, hc, even_w_in[e], even_w_out[e], conv_dw_w[e], conv_ln_g[e], conv_ln_b[e],
                               mlstm_gate_b[e], mlstm_norm_g[e], need_ctx)
        else:
            o = l // 2
            y, yc = odd_mixer(h, hc, odd_w_in[o], odd_w_out[o], pool_w[o], pool_scale[o],
                              q_norm_g[o], k_norm_g[o], cos, sin, need_ctx)
        x = x + mod[2] * rms_norm(y, mix_post_g[l])
        h = modulate(rms_norm(x, ffn_pre_g[l]), mod[3], mod[4])
        x = x + mod[5] * rms_norm(conv_ffn(h, ffn_w_in[l], ffn_conv_w[l], ffn_w_out[l]), ffn_post_g[l])
        if need_ctx:
            xc = xc + mod_c[2] * rms_norm(yc, mix_post_g[l])
            hc = modulate(rms_norm(xc, ffn_pre_g[l]), mod_c[3], mod_c[4])
            xc = xc + mod_c[5] * rms_norm(conv_ffn(hc, ffn_w_in[l], ffn_conv_w[l], ffn_w_out[l]), ffn_post_g[l])
    return x
```

```python
import functools

import jax
import jax.numpy as jnp
import numpy as np
from jax import lax
from jax.experimental import pallas as pl
from jax.experimental.pallas import tpu as pltpu

F32 = jnp.float32
BF16 = jnp.bfloat16

GRID_W = 64
MLSTM_CHUNK = 64
POOL_WINDOWS = (2, 4, 8, 16)
Q_PER_KV = 3
ROPE_THETA = 10000.0
EPS = 1e-6

TM = 256
CONV_HALO = 16
SMALL_HALO = 8
FF_CHUNK = 256
VMEM_LIMIT = 56 * 1024 * 1024


def _cparams(n_axes, vmem=VMEM_LIMIT):
    return pltpu.CompilerParams(dimension_semantics=("arbitrary",) * n_axes, vmem_limit_bytes=vmem)


def _const_spec(shape):
    nd = len(shape)
    return pl.BlockSpec(shape, lambda *_: (0,) * nd)


def _rms(x, g):
    return x * lax.rsqrt(jnp.mean(x * x, axis=-1, keepdims=True) + EPS) * g


def _segment_halo_valid(j, n_tiles_total, has_ctx):
    if has_ctx:
        prev_ok = jnp.logical_and(j != 0, j != 1)
        next_ok = jnp.logical_and(j != 0, j != n_tiles_total - 1)
    else:
        prev_ok = j != 0
        next_ok = j != n_tiles_total - 1
    return prev_ok.astype(F32), next_ok.astype(F32)


def _ada_kernel(c_ref, w_ref, b_ref, o_ref):
    c = c_ref[...]
    s = c * jax.nn.sigmoid(c)
    o_ref[0] = jnp.dot(s, w_ref[0], preferred_element_type=F32, precision=lax.Precision.HIGHEST) + b_ref[0]


def _ada_all(cc, ada_w, ada_b):
    depth, d, n = ada_w.shape
    nb = n // 4
    rows = cc.shape[0]
    return pl.pallas_call(
        _ada_kernel,
        grid=(depth, n // nb),
        in_specs=[pl.BlockSpec((rows, d), lambda l, j: (0, 0)),
                  pl.BlockSpec((1, d, nb), lambda l, j: (l, 0, j)),
                  pl.BlockSpec((1, 1, nb), lambda l, j: (l, 0, j))],
        out_specs=pl.BlockSpec((1, rows, nb), lambda l, j: (l, 0, j)),
        out_shape=jax.ShapeDtypeStruct((depth, rows, n), F32),
        compiler_params=_cparams(2),
        name="ada_mod",
    )(cc, ada_w, ada_b.reshape(depth, 1, n))


def _mod_spec(d, off, n_batch):
    if off == 0:
        return pl.BlockSpec((1, 1, d), lambda b, i: (jnp.where(i == 0, n_batch, b), 0, 0))
    return pl.BlockSpec((1, 1, d), lambda b, i: (b, 0, 0))


def _even_proj_kernel(x_ref, sh_ref, sc_ref, g_ref, w_ref, wg_ref, gb_ref,
                      u_ref, q_ref, k_ref, v_ref, o_ref, gt_ref, *, dc, dm, k_scale, n_sub):
    x = x_ref[0]
    h = (_rms(x, g_ref[...]) * (1.0 + sc_ref[0]) + sh_ref[0]).astype(BF16)
    p = jnp.dot(h, w_ref[...], preferred_element_type=F32)
    u_ref[0] = (p[:, :dc] * jax.nn.sigmoid(p[:, dc:2 * dc])).astype(BF16)
    off = 2 * dc
    q_ref[0] = p[:, off:off + dm].astype(BF16)
    k_ref[0] = (p[:, off + dm:off + 2 * dm] * k_scale).astype(BF16)
    v_ref[0] = p[:, off + 2 * dm:off + 3 * dm].astype(BF16)
    o_ref[0] = jax.nn.sigmoid(p[:, off + 3 * dm:off + 4 * dm]).astype(BF16)
    gt = lax.dot_general(wg_ref[...], h, (((1,), (1,)), ((), ())), preferred_element_type=F32) + gb_ref[...]
    row = lax.broadcasted_iota(jnp.int32, gt.shape, 0)
    gt = jnp.where((row & 2) != 0, jax.nn.log_sigmoid(gt), gt)
    n_pairs = gt.shape[0] // 8
    for pr in range(n_pairs):
        for c in range(n_sub):
            gt_ref[0, pr, c] = gt[pr * 8:(pr + 1) * 8, c * MLSTM_CHUNK:(c + 1) * MLSTM_CHUNK]


def _even_proj(xx, shift, scale, pre_g, w_main, w_gate_t, gate_b_col, *, n_batch, dc, dm, hd):
    bsz, tt, d = xx.shape
    nt = tt // TM
    n_sub = TM // MLSTM_CHUNK
    n_pairs = w_gate_t.shape[0] // 8
    nc = tt // MLSTM_CHUNK
    tok = lambda w: pl.BlockSpec((1, TM, w), lambda b, i: (b, i, 0))
    kern = functools.partial(_even_proj_kernel, dc=dc, dm=dm, k_scale=float(hd) ** -0.5, n_sub=n_sub)
    return pl.pallas_call(
        kern,
        grid=(bsz, nt),
        in_specs=[tok(d), _mod_spec(d, 0, n_batch), _mod_spec(d, 0, n_batch), _const_spec((1, d)),
                  _const_spec(w_main.shape), _const_spec(w_gate_t.shape), _const_spec(gate_b_col.shape)],
        out_specs=[tok(dc), tok(dm), tok(dm), tok(dm), tok(dm),
                   pl.BlockSpec((1, n_pairs, n_sub, 8, MLSTM_CHUNK), lambda b, i: (b, 0, i, 0, 0))],
        out_shape=[jax.ShapeDtypeStruct((bsz, tt, dc), BF16)] + [jax.ShapeDtypeStruct((bsz, tt, dm), BF16)] * 4
        + [jax.ShapeDtypeStruct((bsz, n_pairs, nc, 8, MLSTM_CHUNK), F32)],
        compiler_params=_cparams(2),
        name="even_proj",
    )(xx, shift, scale, pre_g, w_main, w_gate_t, gate_b_col)


def _mlstm_kernel(q_ref, k_ref, v_ref, og_ref, gt_ref, ng_ref, out_ref,
                  hf_ref, hb_ref, c_ref, m_ref, *, hd, n_chunks, n_ctx_chunks, tt):
    L = MLSTM_CHUNK
    c_ref[...] = jnp.zeros(c_ref.shape, F32)
    m_ref[...] = jnp.zeros(m_ref.shape, F32)
    rowi = lax.broadcasted_iota(jnp.int32, (L, L), 0)
    coli = lax.broadcasted_iota(jnp.int32, (L, L), 1)
    eye = rowi == coli
    tri_by_dir = (coli <= rowi, coli >= rowi)
    lane = lax.broadcasted_iota(jnp.int32, (L, hd), 1)
    ones_col = jnp.where(lane == 0, 1.0, 0.0).astype(BF16)

    def step(s, carry):
        for d in range(2):
            if d == 0:
                c = s
            else:
                c = jnp.where(s < n_ctx_chunks, n_ctx_chunks - 1 - s, n_chunks - 1 - (s - n_ctx_chunks))
            r0 = pl.multiple_of(c * L, L)
            gt = gt_ref[0, 0, c]
            tri = tri_by_dir[d]
            h_ref = hf_ref if d == 0 else hb_ref
            for hh in range(2):
                ci = d * 2 + hh
                cols = slice(hh * hd, (hh + 1) * hd)
                i_row = gt[4 * d + hh:4 * d + hh + 1, :]
                lf_row = gt[4 * d + 2 + hh:4 * d + 3 + hh, :]
                b_col = jnp.sum(jnp.where(tri, lf_row, 0.0), axis=1, keepdims=True)
                b_row = jnp.sum(jnp.where(eye, b_col, 0.0), axis=0, keepdims=True)
                g_row = i_row - b_row
                g_col = jnp.sum(jnp.where(eye, g_row, 0.0), axis=1, keepdims=True)
                cg_col = jnp.max(jnp.where(tri, g_row, -jnp.inf), axis=1, keepdims=True)
                b_last = jnp.sum(lf_row, axis=1, keepdims=True)
                g_max = jnp.max(g_row, axis=1, keepdims=True)
                m_prev = m_ref[ci][:1, :1]
                m_col = jnp.maximum(m_prev, cg_col)
                g_top = jnp.maximum(m_prev, g_max)
                decay = jnp.where(tri, jnp.exp(g_row - m_col), 0.0)
                w_col = jnp.exp(m_prev - m_col)
                qh = q_ref[0, pl.ds(r0, L), cols]
                kh = k_ref[0, pl.ds(r0, L), cols]
                vh = v_ref[0, pl.ds(r0, L), cols]
                v_aug = jnp.concatenate([vh, ones_col], axis=1)
                sc = lax.dot_general(qh, kh, (((1,), (1,)), ((), ())), preferred_element_type=F32) * decay
                c_aug = c_ref[ci]
                qc = jnp.dot(qh, c_aug.astype(BF16), preferred_element_type=F32)
                sv = jnp.dot(sc.astype(BF16), v_aug, preferred_element_type=F32)
                num = w_col * qc[:, :hd] + sv[:, :hd]
                den = w_col * qc[:, hd:hd + 1] + sv[:, hd:hd + 1]
                h_ref[pl.ds(r0, L), cols] = num / jnp.maximum(jnp.abs(den), jnp.exp(-(b_col + m_col)))
                e_col = jnp.exp(g_col - g_top)
                ev = (e_col * v_aug.astype(F32)).astype(BF16)
                upd = lax.dot_general(kh, ev, (((0,), (0,)), ((), ())), preferred_element_type=F32)
                c_ref[ci] = jnp.exp(m_prev - g_top) * c_aug + upd
                m_ref[ci] = jnp.broadcast_to(b_last + g_top, m_ref.shape[1:])
        return carry

    lax.fori_loop(0, n_chunks, step, 0)

    def merge(t, carry):
        r0 = pl.multiple_of(t * TM, TM)
        hsum = hf_ref[pl.ds(r0, TM), :] + hb_ref[pl.ds(r0, TM), :]
        og = og_ref[0, pl.ds(r0, TM), :].astype(F32)
        ng = ng_ref[...]
        parts = []
        for hh in range(2):
            cols = slice(hh * hd, (hh + 1) * hd)
            parts.append(_rms(hsum[:, cols], ng[:, cols]))
        out_ref[0, pl.ds(r0, TM), :] = (jnp.concatenate(parts, axis=1) * og).astype(BF16)
        return carry

    lax.fori_loop(0, tt // TM, merge, 0)


def _mlstm(q, k, v, og, gt, norm_g, *, hd, n_ctx_chunks):
    bsz, tt, dm = q.shape
    n_pairs = gt.shape[1]
    nc = gt.shape[2]
    pw = 2 * hd
    tok = pl.BlockSpec((1, tt, pw), lambda b, p: (b, 0, p))
    kern = functools.partial(_mlstm_kernel, hd=hd, n_chunks=nc, n_ctx_chunks=n_ctx_chunks, tt=tt)
    return pl.pallas_call(
        kern,
        grid=(bsz, n_pairs),
        in_specs=[tok, tok, tok, tok,
                  pl.BlockSpec((1, 1, nc, 8, MLSTM_CHUNK), lambda b, p: (b, p, 0, 0, 0)),
                  pl.BlockSpec((1, pw), lambda b, p: (0, p))],
        out_specs=tok,
        out_shape=jax.ShapeDtypeStruct((bsz, tt, dm), BF16),
        scratch_shapes=[pltpu.VMEM((tt, pw), F32), pltpu.VMEM((tt, pw), F32),
                        pltpu.VMEM((4, hd, 2 * hd), F32), pltpu.VMEM((4, 8, 128), F32)],
        compiler_params=_cparams(2),
        name="mlstm",
    )(q, k, v, og, gt, norm_g)


def _even_out_kernel(u_ref, up_ref, un_ref, dw_ref, lg_ref, lb_ref, mm_ref, w_ref, x_ref, gate_ref, pg_ref,
                     o_ref, *, n_tiles_total, width):
    i = pl.program_id(1)
    pv, nv = _segment_halo_valid(i, n_tiles_total, True)
    uext = jnp.concatenate([up_ref[0].astype(F32) * pv, u_ref[0].astype(F32), un_ref[0].astype(F32) * nv], axis=0)
    dw = dw_ref[...]
    half = width // 2
    base = CONV_HALO - half
    span = TM + 8 * ((width - 1) // 8)
    acc = jnp.zeros((TM, uext.shape[1]), F32)
    for r in range(8):
        shifted = uext[base + r:base + r + span]
        for a in range((width - r + 7) // 8):
            j = 8 * a + r
            acc = acc + dw[j:j + 1, :] * shifted[8 * a:8 * a + TM]
    mu = jnp.mean(acc, axis=-1, keepdims=True)
    cen = acc - mu
    var = jnp.mean(cen * cen, axis=-1, keepdims=True)
    y = cen * lax.rsqrt(var + EPS) * lg_ref[...] + lb_ref[...]
    conv_out = (y * jax.nn.sigmoid(y)).astype(BF16)
    cat = jnp.concatenate([conv_out, mm_ref[0]], axis=1)
    z = jnp.dot(cat, w_ref[...], preferred_element_type=F32)
    o_ref[0] = x_ref[0] + gate_ref[0] * _rms(z, pg_ref[...])


def _even_out(u, mm, xx, gate, post_g, dw_w, ln_g, ln_b, w_out, *, n_batch):
    bsz, tt, d = xx.shape
    nt = tt // TM
    dc = u.shape[-1]
    hb = TM // CONV_HALO
    n_hblk = tt // CONV_HALO
    tok = lambda w: pl.BlockSpec((1, TM, w), lambda b, i: (b, i, 0))
    kern = functools.partial(_even_out_kernel, n_tiles_total=nt, width=dw_w.shape[0])
    return pl.pallas_call(
        kern,
        grid=(bsz, nt),
        in_specs=[tok(dc),
                  pl.BlockSpec((1, CONV_HALO, dc), lambda b, i: (b, jnp.maximum(i * hb - 1, 0), 0)),
                  pl.BlockSpec((1, CONV_HALO, dc), lambda b, i: (b, jnp.minimum((i + 1) * hb, n_hblk - 1), 0)),
                  _const_spec(dw_w.shape), _const_spec((1, dc)), _const_spec((1, dc)),
                  tok(mm.shape[-1]), _const_spec(w_out.shape), tok(d), _mod_spec(d, 0, n_batch),
                  _const_spec((1, d))],
        out_specs=tok(d),
        out_shape=jax.ShapeDtypeStruct((bsz, tt, d), F32),
        compiler_params=_cparams(2),
        name="even_out",
    )(u, u, u, dw_w, ln_g, ln_b, mm, w_out, xx, gate, post_g)


def _odd_proj_kernel(x_ref, sh_ref, sc_ref, g_ref, wp_ref, wqk_ref, wv_ref, cos_ref, sin_ref, qg_ref, kg_ref,
                     pool_ref, qt_ref, k_ref, vt_ref, *, hd, n_q, n_kv, q_scale):
    x = x_ref[0]
    h = (_rms(x, g_ref[...]) * (1.0 + sc_ref[0]) + sh_ref[0]).astype(BF16)
    pool_ref[0] = jnp.dot(h, wp_ref[...], preferred_element_type=F32)
    nt_dims = (((1,), (1,)), ((), ()))
    qk_t = lax.dot_general(wqk_ref[...], h, nt_dims, preferred_element_type=F32)
    cos = cos_ref[...]
    sin = sin_ref[...]
    half = hd // 2
    for hi in range(n_q + n_kv):
        t = qk_t[hi * hd:(hi + 1) * hd]
        gain = qg_ref[...] if hi < n_q else kg_ref[...]
        tn = t * lax.rsqrt(jnp.mean(t * t, axis=0, keepdims=True) + EPS) * gain
        x1 = tn[:half]
        x2 = tn[half:]
        rot = jnp.concatenate([x1 * cos - x2 * sin, x1 * sin + x2 * cos], axis=0)
        if hi < n_q:
            kvh, gq = divmod(hi, Q_PER_KV)
            qt_ref[0, kvh, 0, :, gq * TM:(gq + 1) * TM] = (rot * q_scale).astype(BF16)
        else:
            k_ref[0, hi - n_q] = rot.T.astype(BF16)
    v_t = lax.dot_general(wv_ref[...], h, nt_dims, preferred_element_type=F32)
    for kvh in range(n_kv):
        vt_ref[0, kvh] = v_t[kvh * hd:(kvh + 1) * hd].astype(BF16)


def _odd_proj(xx, shift, scale, pre_g, w_pool, w_qk_t, w_v_t, cos_t, sin_t, qg_col, kg_col, *, n_batch, hd, n_q, n_kv):
    bsz, tt, d = xx.shape
    nt = tt // TM
    dp = w_pool.shape[1]
    kern = functools.partial(_odd_proj_kernel, hd=hd, n_q=n_q, n_kv=n_kv, q_scale=float(hd) ** -0.5)
    return pl.pallas_call(
        kern,
        grid=(bsz, nt),
        in_specs=[pl.BlockSpec((1, TM, d), lambda b, i: (b, i, 0)),
                  _mod_spec(d, 0, n_batch), _mod_spec(d, 0, n_batch), _const_spec((1, d)),
                  _const_spec(w_pool.shape), _const_spec(w_qk_t.shape), _const_spec(w_v_t.shape),
                  pl.BlockSpec((hd // 2, TM), lambda b, i: (0, i)), pl.BlockSpec((hd // 2, TM), lambda b, i: (0, i)),
                  _const_spec((hd, 1)), _const_spec((hd, 1))],
        out_specs=[pl.BlockSpec((1, TM, dp), lambda b, i: (b, i, 0)),
                   pl.BlockSpec((1, n_kv, 1, hd, Q_PER_KV * TM), lambda b, i: (b, 0, i, 0, 0)),
                   pl.BlockSpec((1, n_kv, TM, hd), lambda b, i: (b, 0, i, 0)),
                   pl.BlockSpec((1, n_kv, hd, TM), lambda b, i: (b, 0, 0, i))],
        out_shape=[jax.ShapeDtypeStruct((bsz, tt, dp), F32),
                   jax.ShapeDtypeStruct((bsz, n_kv, nt, hd, Q_PER_KV * TM), BF16),
                   jax.ShapeDtypeStruct((bsz, n_kv, tt, hd), BF16),
                   jax.ShapeDtypeStruct((bsz, n_kv, hd, tt), BF16)],
        compiler_params=_cparams(2),
        name="odd_proj",
    )(xx, shift, scale, pre_g, w_pool, w_qk_t, w_v_t, cos_t, sin_t, qg_col, kg_col)


def _attn_kernel(qt_ref, k_ref, vt_ref, o_ref, *, hd, tile_off):
    def attend(k, vt):
        for g in range(Q_PER_KV):
            q_t = qt_ref[0, 0, 0, :, g * TM:(g + 1) * TM]
            s = jnp.dot(k, q_t, preferred_element_type=F32)
            p = jnp.exp(s - jnp.max(s, axis=0, keepdims=True))
            den = jnp.sum(p, axis=0, keepdims=True)
            o_t = jnp.dot(vt, p.astype(BF16), preferred_element_type=F32)
            o_ref[0, :, g * hd:(g + 1) * hd] = (o_t / den).T.astype(BF16)

    if tile_off == 0:
        i = pl.program_id(2)

        @pl.when(i == 0)
        def _():
            attend(k_ref[0, 0, :TM], vt_ref[0, 0, :, :TM])

        @pl.when(i != 0)
        def _():
            attend(k_ref[0, 0], vt_ref[0, 0])
    else:
        attend(k_ref[0, 0], vt_ref[0, 0])


def _attention(q_t, k, v_t, *, tile_off):
    bsz, n_kv, nt, hd, qw = q_t.shape
    tt = k.shape[2]
    nq = nt - tile_off
    kern = functools.partial(_attn_kernel, hd=hd, tile_off=tile_off)
    return pl.pallas_call(
        kern,
        grid=(bsz, n_kv, nq),
        in_specs=[pl.BlockSpec((1, 1, 1, hd, qw), lambda b, h, i: (b, h, i + tile_off, 0, 0)),
                  pl.BlockSpec((1, 1, tt, hd), lambda b, h, i: (b, h, 0, 0)),
                  pl.BlockSpec((1, 1, hd, tt), lambda b, h, i: (b, h, 0, 0))],
        out_specs=pl.BlockSpec((1, TM, Q_PER_KV * hd), lambda b, h, i: (b, i, h)),
        out_shape=jax.ShapeDtypeStruct((bsz, nq * TM, n_kv * Q_PER_KV * hd), BF16),
        compiler_params=_cparams(3),
        name="gqa_attention",
    )(q_t, k, v_t)


def _odd_out_kernel(u_ref, up_ref, un_ref, pw_ref, ps_ref, at_ref, w_ref, x_ref, gate_ref, pg_ref, o_ref,
                    *, n_tiles_total, tile_off, pool_group, seq_latent):
    i = pl.program_id(1)
    j = i + tile_off
    pv, nv = _segment_halo_valid(j, n_tiles_total, True)
    u = u_ref[0]
    uext = jnp.concatenate([up_ref[0] * pv, u, un_ref[0] * nv], axis=0)
    n_ext = uext.shape[0]
    s2 = uext[:n_ext - 1] + uext[1:]
    s4 = s2[:n_ext - 3] + s2[2:]
    s8 = s4[:n_ext - 7] + s4[4:]
    s16 = s8[:n_ext - 15] + s8[8:]
    sums = (s2, s4, s8, s16)
    seg_start = jnp.where(j == 0, 0, 1)
    seg_len = jnp.where(j == 0, TM, seq_latent).astype(F32)
    t = ((j - seg_start) * TM + lax.broadcasted_iota(jnp.int32, (TM, 1), 0)).astype(F32)
    lane_group = lax.broadcasted_iota(jnp.int32, (1, u.shape[1]), 1) // pool_group
    mean = jnp.zeros_like(u)
    for gi, w in enumerate(POOL_WINDOWS):
        hw = w // 2
        win = sums[gi][SMALL_HALO - hw:SMALL_HALO - hw + TM]
        cnt = jnp.minimum(t + hw, seg_len) - jnp.maximum(t - hw, 0.0)
        mean = jnp.where(lane_group == gi, win / cnt, mean)
    dpool = (mean - u).astype(BF16)
    y = jnp.dot(dpool, pw_ref[...], preferred_element_type=F32) * ps_ref[...]
    cat = jnp.concatenate([y.astype(BF16), at_ref[0]], axis=1)
    z = jnp.dot(cat, w_ref[...], preferred_element_type=F32)
    o_ref[0] = x_ref[0] + gate_ref[0] * _rms(z, pg_ref[...])


def _odd_out(pool_u, attn, xx, gate, post_g, pool_bd, pool_scale, w_out, *, n_batch, tile_off, seq_latent):
    bsz, tt, d = xx.shape
    nt = tt // TM
    n_out = nt - tile_off
    dp = pool_u.shape[-1]
    hb = TM // SMALL_HALO
    n_hblk = tt // SMALL_HALO
    kern = functools.partial(_odd_out_kernel, n_tiles_total=nt, tile_off=tile_off,
                             pool_group=dp // len(POOL_WINDOWS), seq_latent=seq_latent)
    return pl.pallas_call(
        kern,
        grid=(bsz, n_out),
        in_specs=[pl.BlockSpec((1, TM, dp), lambda b, i: (b, i + tile_off, 0)),
                  pl.BlockSpec((1, SMALL_HALO, dp), lambda b, i: (b, jnp.maximum((i + tile_off) * hb - 1, 0), 0)),
                  pl.BlockSpec((1, SMALL_HALO, dp),
                               lambda b, i: (b, jnp.minimum((i + tile_off + 1) * hb, n_hblk - 1), 0)),
                  _const_spec(pool_bd.shape), _const_spec((1, dp)),
                  pl.BlockSpec((1, TM, attn.shape[-1]), lambda b, i: (b, i, 0)),
                  _const_spec(w_out.shape),
                  pl.BlockSpec((1, TM, d), lambda b, i: (b, i + tile_off, 0)),
                  _mod_spec(d, tile_off, n_batch), _const_spec((1, d))],
        out_specs=pl.BlockSpec((1, TM, d), lambda b, i: (b, i, 0)),
        out_shape=jax.ShapeDtypeStruct((bsz, n_out * TM, d), F32),
        compiler_params=_cparams(2),
        name="odd_out",
    )(pool_u, pool_u, pool_u, pool_bd, pool_scale, attn, w_out, xx, gate, post_g)


def _ffn_kernel(x_ref, xp_ref, xn_ref, sh_ref, sc_ref, gate_ref, g_ref, pg_ref, wi_ref, cw_ref, wo_ref, o_ref,
                *, n_tiles_total, has_ctx, n_chunks):
    i = pl.program_id(1)
    pv, nv = _segment_halo_valid(i, n_tiles_total, has_ctx)
    xc = x_ref[0]
    xe = jnp.concatenate([xp_ref[0], xc, xn_ref[0]], axis=0)
    he = _rms(xe, g_ref[...]) * (1.0 + sc_ref[0]) + sh_ref[0]
    row = lax.broadcasted_iota(jnp.int32, (xe.shape[0], 1), 0)
    keep = jnp.where(row < SMALL_HALO, pv, jnp.where(row >= SMALL_HALO + TM, nv, 1.0))
    he = (he * keep).astype(BF16)
    hc = he[SMALL_HALO:SMALL_HALO + TM]
    acc = jnp.zeros((TM, o_ref.shape[2]), F32)
    for c in range(n_chunks):
        g = jnp.dot(he, wi_ref[c], preferred_element_type=F32)
        v = jnp.dot(hc, wi_ref[n_chunks + c], preferred_element_type=F32)
        cw = cw_ref[c]
        gc = (cw[0:1] * g[SMALL_HALO - 1:SMALL_HALO - 1 + TM] + cw[1:2] * g[SMALL_HALO:SMALL_HALO + TM]
              + cw[2:3] * g[SMALL_HALO + 1:SMALL_HALO + 1 + TM])
        u = (gc * jax.nn.sigmoid(gc) * v).astype(BF16)
        acc = acc + jnp.dot(u, wo_ref[c], preferred_element_type=F32)
    o_ref[0] = xc + gate_ref[0] * _rms(acc, pg_ref[...])


def _ffn(xx, shift, scale, gate, pre_g, post_g, w_in_c, conv_w_c, w_out_c, *, n_batch, has_ctx):
    bsz, tt, d = xx.shape
    nt = tt // TM
    hb = TM // SMALL_HALO
    n_hblk = tt // SMALL_HALO
    n_chunks = w_out_c.shape[0]
    off = 0 if has_ctx else 1
    kern = functools.partial(_ffn_kernel, n_tiles_total=nt, has_ctx=has_ctx, n_chunks=n_chunks)
    return pl.pallas_call(
        kern,
        grid=(bsz, nt),
        in_specs=[pl.BlockSpec((1, TM, d), lambda b, i: (b, i, 0)),
                  pl.BlockSpec((1, SMALL_HALO, d), lambda b, i: (b, jnp.maximum(i * hb - 1, 0), 0)),
                  pl.BlockSpec((1, SMALL_HALO, d), lambda b, i: (b, jnp.minimum((i + 1) * hb, n_hblk - 1), 0)),
                  _mod_spec(d, off, n_batch), _mod_spec(d, off, n_batch), _mod_spec(d, off, n_batch),
                  _const_spec((1, d)), _const_spec((1, d)),
                  _const_spec(w_in_c.shape), _const_spec(conv_w_c.shape), _const_spec(w_out_c.shape)],
        out_specs=pl.BlockSpec((1, TM, d), lambda b, i: (b, i, 0)),
        out_shape=jax.ShapeDtypeStruct((bsz, tt, d), F32),
        compiler_params=_cparams(2),
        name="conv_ffn",
    )(xx, xx, xx, shift, scale, gate, pre_g, post_g, w_in_c, conv_w_c, w_out_c)


def _rope_tables_t(seq, ctx_len, hd):
    rows = seq // GRID_W
    row = jnp.repeat(jnp.arange(rows), GRID_W).astype(F32)
    col = jnp.tile(jnp.arange(GRID_W), rows).astype(F32)
    n_freq = hd // 4
    inv = ROPE_THETA ** (-jnp.arange(n_freq, dtype=F32) / n_freq)
    ang = jnp.concatenate([row[:, None] * inv, col[:, None] * inv], axis=-1)
    cos = jnp.concatenate([jnp.ones((ctx_len, hd // 2), F32), jnp.cos(ang)], axis=0)
    sin = jnp.concatenate([jnp.zeros((ctx_len, hd // 2), F32), jnp.sin(ang)], axis=0)
    return cos.T, sin.T


def kernel(x, c, ctx, c_ctx, ada_w, ada_b, mix_pre_g, mix_post_g, ffn_pre_g, ffn_post_g, ffn_w_in, ffn_conv_w,
           ffn_w_out, even_w_in, even_w_out, conv_dw_w, conv_ln_g, conv_ln_b, mlstm_gate_b, mlstm_norm_g,
           odd_w_in, odd_w_out, pool_w, pool_scale, q_norm_g, k_norm_g):
    bsz, seq, d = x.shape
    ctx_len = ctx.shape[1]
    depth = ada_w.shape[0]
    assert ctx_len == TM and seq % TM == 0 and seq % GRID_W == 0
    dc = conv_dw_w.shape[2]
    dm = mlstm_norm_g.shape[1]
    n_heads = mlstm_gate_b.shape[2]
    hd_m = dm // n_heads
    assert n_heads % 2 == 0 and even_w_in.shape[2] == 2 * dc + 4 * dm + 4 * n_heads
    hd = q_norm_g.shape[1]
    dp = pool_scale.shape[1]
    n_kv = (odd_w_in.shape[2] - dp - (odd_w_out.shape[1] - dp)) // (2 * hd)
    n_q = (odd_w_out.shape[1] - dp) // hd
    assert n_q == n_kv * Q_PER_KV
    d_ff = ffn_w_out.shape[1]
    n_ff = d_ff // FF_CHUNK
    assert d_ff % FF_CHUNK == 0

    pad = (-(bsz + 1)) % 8
    cc = jnp.concatenate([c, c_ctx[None, :], jnp.zeros((pad, d), F32)], axis=0)
    mods = _ada_all(cc, ada_w, ada_b)

    def mod(l, k):
        return mods[l, :bsz + 1, k * d:(k + 1) * d].reshape(bsz + 1, 1, d)

    cos_t, sin_t = _rope_tables_t(seq, ctx_len, hd)
    perm = np.concatenate([np.arange(0, hd, 2), np.arange(1, hd, 2)])

    xx = jnp.concatenate([ctx, x], axis=1)
    row2 = lambda a: a.reshape(1, -1)

    for l in range(depth):
        last = l == depth - 1
        if l % 2 == 0:
            e = l // 2
            w_in = even_w_in[e]
            n_main = 2 * dc + 4 * dm
            w_main = w_in[:, :n_main].astype(BF16)
            wg = w_in[:, n_main:].reshape(d, 4, n_heads // 2, 2)
            w_gate_t = wg.transpose(2, 1, 3, 0).reshape(4 * n_heads, d).astype(BF16)
            gate_b_col = mlstm_gate_b[e].reshape(4, n_heads // 2, 2).transpose(1, 0, 2).reshape(4 * n_heads, 1)
            u, q, k, v, og, gt = _even_proj(xx, mod(l, 0), mod(l, 1), row2(mix_pre_g[l]), w_main, w_gate_t,
                                            gate_b_col, n_batch=bsz, dc=dc, dm=dm, hd=hd_m)
            mm = _mlstm(q, k, v, og, gt, row2(mlstm_norm_g[e]), hd=hd_m, n_ctx_chunks=ctx_len // MLSTM_CHUNK)
            xx = _even_out(u, mm, xx, mod(l, 2), row2(mix_post_g[l]), conv_dw_w[e], row2(conv_ln_g[e]),
                           row2(conv_ln_b[e]), even_w_out[e].astype(BF16), n_batch=bsz)
        else:
            o = l // 2
            w_in = odd_w_in[o]
            w_pool = w_in[:, :dp].astype(BF16)
            w_qk = w_in[:, dp:dp + (n_q + n_kv) * hd].reshape(d, n_q + n_kv, hd)[:, :, perm]
            w_qk_t = w_qk.reshape(d, -1).T.astype(BF16)
            w_v_t = w_in[:, dp + (n_q + n_kv) * hd:].T.astype(BF16)
            pool_u, q_t, k, v_t = _odd_proj(xx, mod(l, 0), mod(l, 1), row2(mix_pre_g[l]), w_pool, w_qk_t, w_v_t,
                                            cos_t, sin_t, q_norm_g[o][perm].reshape(hd, 1),
                                            k_norm_g[o][perm].reshape(hd, 1), n_batch=bsz, hd=hd, n_q=n_q, n_kv=n_kv)
            tile_off = 1 if last else 0
            attn = _attention(q_t, k, v_t, tile_off=tile_off)
            pg = dp // len(POOL_WINDOWS)
            pool_bd = jnp.zeros((dp, dp), F32)
            for gi in range(len(POOL_WINDOWS)):
                pool_bd = pool_bd.at[gi * pg:(gi + 1) * pg, gi * pg:(gi + 1) * pg].set(pool_w[o, gi])
            xx = _odd_out(pool_u, attn, xx, mod(l, 2), row2(mix_post_g[l]), pool_bd.astype(BF16),
                          row2(pool_scale[o]), odd_w_out[o].astype(BF16), n_batch=bsz, tile_off=tile_off,
                          seq_latent=seq)
        w_in_f = ffn_w_in[l].astype(BF16).reshape(d, 2 * n_ff, FF_CHUNK).transpose(1, 0, 2)
        conv_c = ffn_conv_w[l].reshape(-1, n_ff, FF_CHUNK).transpose(1, 0, 2)
        w_out_f = ffn_w_out[l].astype(BF16).reshape(n_ff, FF_CHUNK, d)
        xx = _ffn(xx, mod(l, 3), mod(l, 4), mod(l, 5), row2(ffn_pre_g[l]), row2(ffn_post_g[l]),
                  w_in_f, conv_c, w_out_f, n_batch=bsz, has_ctx=xx.shape[1] != seq)
    return xx if xx.shape[1] == seq else xx[:, ctx_len:]
```

```python
import functools

import jax
import jax.numpy as jnp
import numpy as np
from jax import lax
from jax.experimental import pallas as pl
from jax.experimental.pallas import tpu as pltpu

F32 = jnp.float32
BF16 = jnp.bfloat16

GRID_W = 64
MLSTM_CHUNK = 64
POOL_WINDOWS = (2, 4, 8, 16)
Q_PER_KV = 3
ROPE_THETA = 10000.0
EPS = 1e-6
LOG2_E = 1.4426950408889634

TM = 256
CONV_HALO = 16
SMALL_HALO = 8
FF_CHUNK = 256
VMEM_LIMIT = 56 * 1024 * 1024


def _cparams(n_axes, vmem=VMEM_LIMIT):
    return pltpu.CompilerParams(dimension_semantics=("arbitrary",) * n_axes, vmem_limit_bytes=vmem)


def _const_spec(shape):
    nd = len(shape)
    return pl.BlockSpec(shape, lambda *_: (0,) * nd)


def _rms(x, g):
    return x * lax.rsqrt(jnp.mean(x * x, axis=-1, keepdims=True) + EPS) * g


def _segment_halo_valid(j, n_tiles_total, has_ctx):
    if has_ctx:
        prev_ok = jnp.logical_and(j != 0, j != 1)
        next_ok = jnp.logical_and(j != 0, j != n_tiles_total - 1)
    else:
        prev_ok = j != 0
        next_ok = j != n_tiles_total - 1
    return prev_ok.astype(F32), next_ok.astype(F32)


def _ada_kernel(c_ref, w_ref, b_ref, o_ref):
    c = c_ref[...]
    s = c * jax.nn.sigmoid(c)
    o_ref[0] = jnp.dot(s, w_ref[0], preferred_element_type=F32, precision=lax.Precision.HIGHEST) + b_ref[0]


def _ada_all(cc, ada_w, ada_b):
    depth, d, n = ada_w.shape
    nb = n // 4
    rows = cc.shape[0]
    return pl.pallas_call(
        _ada_kernel,
        grid=(depth, n // nb),
        in_specs=[pl.BlockSpec((rows, d), lambda l, j: (0, 0)),
                  pl.BlockSpec((1, d, nb), lambda l, j: (l, 0, j)),
                  pl.BlockSpec((1, 1, nb), lambda l, j: (l, 0, j))],
        out_specs=pl.BlockSpec((1, rows, nb), lambda l, j: (l, 0, j)),
        out_shape=jax.ShapeDtypeStruct((depth, rows, n), F32),
        compiler_params=_cparams(2),
        name="ada_mod",
    )(cc, ada_w, ada_b.reshape(depth, 1, n))


def _mod_spec(d, off, n_batch):
    if off == 0:
        return pl.BlockSpec((1, 1, d), lambda b, i: (jnp.where(i == 0, n_batch, b), 0, 0))
    return pl.BlockSpec((1, 1, d), lambda b, i: (b, 0, 0))


def _even_proj_kernel(x_ref, sh_ref, sc_ref, g_ref, w_ref, wg_ref, gb_ref,
                      u_ref, q_ref, k_ref, v_ref, o_ref, gt_ref, *, dc, dm, k_scale, n_sub):
    x = x_ref[0]
    h = (_rms(x, g_ref[...]) * (1.0 + sc_ref[0]) + sh_ref[0]).astype(BF16)
    p = jnp.dot(h, w_ref[...], preferred_element_type=F32)
    u_ref[0] = (p[:, :dc] * jax.nn.sigmoid(p[:, dc:2 * dc])).astype(BF16)
    off = 2 * dc
    q_ref[0] = p[:, off:off + dm].astype(BF16)
    k_ref[0] = (p[:, off + dm:off + 2 * dm] * k_scale).astype(BF16)
    v_ref[0] = p[:, off + 2 * dm:off + 3 * dm].astype(BF16)
    o_ref[0] = jax.nn.sigmoid(p[:, off + 3 * dm:off + 4 * dm]).astype(BF16)
    gt = lax.dot_general(wg_ref[...], h, (((1,), (1,)), ((), ())), preferred_element_type=F32) + gb_ref[...]
    row = lax.broadcasted_iota(jnp.int32, gt.shape, 0)
    gt = jnp.where((row & 2) != 0, jax.nn.log_sigmoid(gt), gt)
    n_pairs = gt.shape[0] // 8
    for pr in range(n_pairs):
        for c in range(n_sub):
            gt_ref[0, pr, c] = gt[pr * 8:(pr + 1) * 8, c * MLSTM_CHUNK:(c + 1) * MLSTM_CHUNK]


def _even_proj(xx, shift, scale, pre_g, w_main, w_gate_t, gate_b_col, *, n_batch, dc, dm, hd):
    bsz, tt, d = xx.shape
    nt = tt // TM
    n_sub = TM // MLSTM_CHUNK
    n_pairs = w_gate_t.shape[0] // 8
    nc = tt // MLSTM_CHUNK
    tok = lambda w: pl.BlockSpec((1, TM, w), lambda b, i: (b, i, 0))
    kern = functools.partial(_even_proj_kernel, dc=dc, dm=dm, k_scale=float(hd) ** -0.5, n_sub=n_sub)
    return pl.pallas_call(
        kern,
        grid=(bsz, nt),
        in_specs=[tok(d), _mod_spec(d, 0, n_batch), _mod_spec(d, 0, n_batch), _const_spec((1, d)),
                  _const_spec(w_main.shape), _const_spec(w_gate_t.shape), _const_spec(gate_b_col.shape)],
        out_specs=[tok(dc), tok(dm), tok(dm), tok(dm), tok(dm),
                   pl.BlockSpec((1, n_pairs, n_sub, 8, MLSTM_CHUNK), lambda b, i: (b, 0, i, 0, 0))],
        out_shape=[jax.ShapeDtypeStruct((bsz, tt, dc), BF16)] + [jax.ShapeDtypeStruct((bsz, tt, dm), BF16)] * 4
        + [jax.ShapeDtypeStruct((bsz, n_pairs, nc, 8, MLSTM_CHUNK), F32)],
        compiler_params=_cparams(2),
        name="even_proj",
    )(xx, shift, scale, pre_g, w_main, w_gate_t, gate_b_col)


def _mlstm_kernel(q_ref, k_ref, v_ref, og_ref, gt_ref, ng_ref, out_ref,
                  hf_ref, hb_ref, c_ref, grow_ref, gmax_ref, blast_ref, mprev_ref,
                  *, hd, n_chunks, n_ctx_chunks, tt):
    L = MLSTM_CHUNK
    rowi = lax.broadcasted_iota(jnp.int32, (L, L), 0)
    coli = lax.broadcasted_iota(jnp.int32, (L, L), 1)
    eye = rowi == coli
    tri_by_dir = (coli <= rowi, coli >= rowi)
    ones_blk = jnp.ones((L, hd), BF16)

    def chunk_of(d, s):
        if d == 0:
            return s
        return jnp.where(s < n_ctx_chunks, n_ctx_chunks - 1 - s, n_chunks - 1 - (s - n_ctx_chunks))

    for d in range(2):
        cum = jnp.where(rowi <= coli if d == 0 else rowi >= coli, 1.0, 0.0)
        for hh in range(2):
            ci = d * 2 + hh
            i_rows = gt_ref[0, 0, :, 4 * d + hh, :]
            lf_rows = gt_ref[0, 0, :, 4 * d + 2 + hh, :]
            b_rows = jnp.dot(lf_rows, cum, preferred_element_type=F32, precision=lax.Precision.HIGHEST)
            g_rows = i_rows - b_rows
            grow_ref[ci, :n_chunks, :] = g_rows
            gmax_ref[ci, :n_chunks, :] = jnp.broadcast_to(jnp.max(g_rows, axis=1, keepdims=True), (n_chunks, 128))
            blast_ref[ci, :n_chunks, :] = jnp.broadcast_to(jnp.sum(lf_rows, axis=1, keepdims=True), (n_chunks, 128))

    def scan(s, ms):
        out = []
        for d in range(2):
            c = chunk_of(d, s)
            for hh in range(2):
                ci = d * 2 + hh
                mprev_ref[ci, pl.ds(c, 1), :] = ms[ci]
                out.append(blast_ref[ci, pl.ds(c, 1), :] + jnp.maximum(ms[ci], gmax_ref[ci, pl.ds(c, 1), :]))
        return tuple(out)

    lax.fori_loop(0, n_chunks, scan, (jnp.zeros((1, 128), F32),) * 4)

    c_ref[...] = jnp.zeros(c_ref.shape, F32)

    def step(s, carry):
        chains = []
        for d in range(2):
            c = chunk_of(d, s)
            r0 = pl.multiple_of(c * L, L)
            gt = gt_ref[0, 0, c]
            tri = tri_by_dir[d]
            for hh in range(2):
                ci = d * 2 + hh
                cols = slice(hh * hd, (hh + 1) * hd)
                lf_row = gt[4 * d + 2 + hh:4 * d + 3 + hh, :]
                g_row = grow_ref[ci, pl.ds(c, 1), :]
                m_prev = mprev_ref[ci, pl.ds(c, 1), :][:, :1]
                g_max = gmax_ref[ci, pl.ds(c, 1), :][:, :1]
                b_col = jnp.sum(jnp.where(tri, lf_row, 0.0), axis=1, keepdims=True)
                g_col = jnp.sum(jnp.where(eye, g_row, 0.0), axis=1, keepdims=True)
                cg_col = jnp.max(jnp.where(tri, g_row, -jnp.inf), axis=1, keepdims=True)
                m_col = jnp.maximum(m_prev, cg_col)
                g_top = jnp.maximum(m_prev, g_max)
                qh = q_ref[0, pl.ds(r0, L), cols]
                kh = k_ref[0, pl.ds(r0, L), cols]
                v_aug = jnp.concatenate([v_ref[0, pl.ds(r0, L), cols], ones_blk], axis=1)
                chains.append(dict(
                    ci=ci, r0=r0, cols=cols, h_ref=hf_ref if d == 0 else hb_ref, qh=qh, kh=kh, v_aug=v_aug,
                    decay=jnp.where(tri, jnp.exp(g_row - m_col), 0.0), w_col=jnp.exp(m_prev - m_col),
                    clamp=jnp.exp(-(b_col + m_col)), keep=jnp.exp(m_prev - g_top),
                    ev=(jnp.exp(g_col - g_top) * v_aug.astype(F32)).astype(BF16)))
        for ch in chains:
            ch["qk"] = lax.dot_general(ch["qh"], ch["kh"], (((1,), (1,)), ((), ())), preferred_element_type=F32)
        for ch in chains:
            ch["c_aug"] = c_ref[ch["ci"]]
            ch["qc"] = jnp.dot(ch["qh"], ch["c_aug"].astype(BF16), preferred_element_type=F32)
        for ch in chains:
            ch["upd"] = lax.dot_general(ch["kh"], ch["ev"], (((0,), (0,)), ((), ())), preferred_element_type=F32)
        for ch in chains:
            ch["sv"] = jnp.dot((ch["qk"] * ch["decay"]).astype(BF16), ch["v_aug"], preferred_element_type=F32)
        for ch in chains:
            tot = ch["w_col"] * ch["qc"] + ch["sv"]
            ch["h_ref"][pl.ds(ch["r0"], L), ch["cols"]] = tot[:, :hd] / jnp.maximum(jnp.abs(tot[:, hd:]), ch["clamp"])
            c_ref[ch["ci"]] = ch["keep"] * ch["c_aug"] + ch["upd"]
        return carry

    lax.fori_loop(0, n_chunks, step, 0, unroll=2)

    def merge(t, carry):
        r0 = pl.multiple_of(t * TM, TM)
        hsum = hf_ref[pl.ds(r0, TM), :] + hb_ref[pl.ds(r0, TM), :]
        og = og_ref[0, pl.ds(r0, TM), :].astype(F32)
        ng = ng_ref[...]
        parts = []
        for hh in range(2):
            cols = slice(hh * hd, (hh + 1) * hd)
            parts.append(_rms(hsum[:, cols], ng[:, cols]))
        out_ref[0, pl.ds(r0, TM), :] = (jnp.concatenate(parts, axis=1) * og).astype(BF16)
        return carry

    lax.fori_loop(0, tt // TM, merge, 0)


def _mlstm(q, k, v, og, gt, norm_g, *, hd, n_ctx_chunks):
    bsz, tt, dm = q.shape
    n_pairs = gt.shape[1]
    nc = gt.shape[2]
    pw = 2 * hd
    nc_pad = -(-nc // 8) * 8
    tok = pl.BlockSpec((1, tt, pw), lambda b, p: (b, 0, p))
    kern = functools.partial(_mlstm_kernel, hd=hd, n_chunks=nc, n_ctx_chunks=n_ctx_chunks, tt=tt)
    return pl.pallas_call(
        kern,
        grid=(bsz, n_pairs),
        in_specs=[tok, tok, tok, tok,
                  pl.BlockSpec((1, 1, nc, 8, MLSTM_CHUNK), lambda b, p: (b, p, 0, 0, 0)),
                  pl.BlockSpec((1, pw), lambda b, p: (0, p))],
        out_specs=tok,
        out_shape=jax.ShapeDtypeStruct((bsz, tt, dm), BF16),
        scratch_shapes=[pltpu.VMEM((tt, pw), F32), pltpu.VMEM((tt, pw), F32),
                        pltpu.VMEM((4, hd, 2 * hd), F32), pltpu.VMEM((4, nc_pad, MLSTM_CHUNK), F32),
                        pltpu.VMEM((4, nc_pad, 128), F32), pltpu.VMEM((4, nc_pad, 128), F32),
                        pltpu.VMEM((4, nc_pad, 128), F32)],
        compiler_params=_cparams(2),
        name="mlstm",
    )(q, k, v, og, gt, norm_g)


def _even_out_kernel(u_ref, up_ref, un_ref, dw_ref, lg_ref, lb_ref, mm_ref, w_ref, x_ref, gate_ref, pg_ref,
                     o_ref, *, n_tiles_total, width):
    i = pl.program_id(1)
    pv, nv = _segment_halo_valid(i, n_tiles_total, True)
    uext = jnp.concatenate([up_ref[0].astype(F32) * pv, u_ref[0].astype(F32), un_ref[0].astype(F32) * nv], axis=0)
    dw = dw_ref[...]
    half = width // 2
    base = CONV_HALO - half
    span = TM + 8 * ((width - 1) // 8)
    acc = jnp.zeros((TM, uext.shape[1]), F32)
    for r in range(8):
        shifted = uext[base + r:base + r + span]
        for a in range((width - r + 7) // 8):
            j = 8 * a + r
            acc = acc + dw[j:j + 1, :] * shifted[8 * a:8 * a + TM]
    mu = jnp.mean(acc, axis=-1, keepdims=True)
    cen = acc - mu
    var = jnp.mean(cen * cen, axis=-1, keepdims=True)
    y = cen * lax.rsqrt(var + EPS) * lg_ref[...] + lb_ref[...]
    conv_out = (y * jax.nn.sigmoid(y)).astype(BF16)
    cat = jnp.concatenate([conv_out, mm_ref[0]], axis=1)
    z = jnp.dot(cat, w_ref[...], preferred_element_type=F32)
    o_ref[0] = x_ref[0] + gate_ref[0] * _rms(z, pg_ref[...])


def _even_out(u, mm, xx, gate, post_g, dw_w, ln_g, ln_b, w_out, *, n_batch):
    bsz, tt, d = xx.shape
    nt = tt // TM
    dc = u.shape[-1]
    hb = TM // CONV_HALO
    n_hblk = tt // CONV_HALO
    tok = lambda w: pl.BlockSpec((1, TM, w), lambda b, i: (b, i, 0))
    kern = functools.partial(_even_out_kernel, n_tiles_total=nt, width=dw_w.shape[0])
    return pl.pallas_call(
        kern,
        grid=(bsz, nt),
        in_specs=[tok(dc),
                  pl.BlockSpec((1, CONV_HALO, dc), lambda b, i: (b, jnp.maximum(i * hb - 1, 0), 0)),
                  pl.BlockSpec((1, CONV_HALO, dc), lambda b, i: (b, jnp.minimum((i + 1) * hb, n_hblk - 1), 0)),
                  _const_spec(dw_w.shape), _const_spec((1, dc)), _const_spec((1, dc)),
                  tok(mm.shape[-1]), _const_spec(w_out.shape), tok(d), _mod_spec(d, 0, n_batch),
                  _const_spec((1, d))],
        out_specs=tok(d),
        out_shape=jax.ShapeDtypeStruct((bsz, tt, d), F32),
        compiler_params=_cparams(2),
        name="even_out",
    )(u, u, u, dw_w, ln_g, ln_b, mm, w_out, xx, gate, post_g)


def _odd_proj_kernel(x_ref, sh_ref, sc_ref, g_ref, wp_ref, wqk_ref, wv_ref, cos_ref, sin_ref, qg_ref, kg_ref,
                     pool_ref, qt_ref, k_ref, vt_ref, *, hd, n_q, n_kv, q_scale):
    x = x_ref[0]
    h = (_rms(x, g_ref[...]) * (1.0 + sc_ref[0]) + sh_ref[0]).astype(BF16)
    pool_ref[0] = jnp.dot(h, wp_ref[...], preferred_element_type=F32)
    nt_dims = (((1,), (1,)), ((), ()))
    qk_t = lax.dot_general(wqk_ref[...], h, nt_dims, preferred_element_type=F32)
    cos = cos_ref[...]
    sin = sin_ref[...]
    half = hd // 2
    for hi in range(n_q + n_kv):
        t = qk_t[hi * hd:(hi + 1) * hd]
        gain = qg_ref[...] if hi < n_q else kg_ref[...]
        tn = t * lax.rsqrt(jnp.mean(t * t, axis=0, keepdims=True) + EPS) * gain
        x1 = tn[:half]
        x2 = tn[half:]
        rot = jnp.concatenate([x1 * cos - x2 * sin, x1 * sin + x2 * cos], axis=0)
        if hi < n_q:
            kvh, gq = divmod(hi, Q_PER_KV)
            qt_ref[0, kvh, 0, :, gq * TM:(gq + 1) * TM] = (rot * q_scale).astype(BF16)
        else:
            k_ref[0, hi - n_q] = rot.T.astype(BF16)
    v_t = lax.dot_general(wv_ref[...], h, nt_dims, preferred_element_type=F32)
    for kvh in range(n_kv):
        vt_ref[0, kvh] = v_t[kvh * hd:(kvh + 1) * hd].astype(BF16)


def _odd_proj(xx, shift, scale, pre_g, w_pool, w_qk_t, w_v_t, cos_t, sin_t, qg_col, kg_col, *, n_batch, hd, n_q, n_kv):
    bsz, tt, d = xx.shape
    nt = tt // TM
    dp = w_pool.shape[1]
    kern = functools.partial(_odd_proj_kernel, hd=hd, n_q=n_q, n_kv=n_kv, q_scale=float(hd) ** -0.5 * LOG2_E)
    return pl.pallas_call(
        kern,
        grid=(bsz, nt),
        in_specs=[pl.BlockSpec((1, TM, d), lambda b, i: (b, i, 0)),
                  _mod_spec(d, 0, n_batch), _mod_spec(d, 0, n_batch), _const_spec((1, d)),
                  _const_spec(w_pool.shape), _const_spec(w_qk_t.shape), _const_spec(w_v_t.shape),
                  pl.BlockSpec((hd // 2, TM), lambda b, i: (0, i)), pl.BlockSpec((hd // 2, TM), lambda b, i: (0, i)),
                  _const_spec((hd, 1)), _const_spec((hd, 1))],
        out_specs=[pl.BlockSpec((1, TM, dp), lambda b, i: (b, i, 0)),
                   pl.BlockSpec((1, n_kv, 1, hd, Q_PER_KV * TM), lambda b, i: (b, 0, i, 0, 0)),
                   pl.BlockSpec((1, n_kv, TM, hd), lambda b, i: (b, 0, i, 0)),
                   pl.BlockSpec((1, n_kv, hd, TM), lambda b, i: (b, 0, 0, i))],
        out_shape=[jax.ShapeDtypeStruct((bsz, tt, dp), F32),
                   jax.ShapeDtypeStruct((bsz, n_kv, nt, hd, Q_PER_KV * TM), BF16),
                   jax.ShapeDtypeStruct((bsz, n_kv, tt, hd), BF16),
                   jax.ShapeDtypeStruct((bsz, n_kv, hd, tt), BF16)],
        compiler_params=_cparams(2),
        name="odd_proj",
    )(xx, shift, scale, pre_g, w_pool, w_qk_t, w_v_t, cos_t, sin_t, qg_col, kg_col)


def _attn_kernel(qt_ref, k_ref, vt_ref, o_ref, *, hd, tile_off):
    def attend(k, vt):
        def scores(g):
            return jnp.dot(k, qt_ref[0, 0, 0, :, g * TM:(g + 1) * TM], preferred_element_type=F32)

        s_next = scores(0)
        for g in range(Q_PER_KV):
            s = s_next
            if g + 1 < Q_PER_KV:
                s_next = scores(g + 1)
            p = jnp.exp2(s - jnp.max(s, axis=0, keepdims=True))
            den = jnp.sum(p, axis=0, keepdims=True)
            o_t = jnp.dot(vt, p.astype(BF16), preferred_element_type=F32)
            o_ref[0, :, g * hd:(g + 1) * hd] = (o_t / den).T.astype(BF16)

    if tile_off == 0:
        i = pl.program_id(2)

        @pl.when(i == 0)
        def _():
            attend(k_ref[0, 0, :TM], vt_ref[0, 0, :, :TM])

        @pl.when(i != 0)
        def _():
            attend(k_ref[0, 0], vt_ref[0, 0])
    else:
        attend(k_ref[0, 0], vt_ref[0, 0])


def _attention(q_t, k, v_t, *, tile_off):
    bsz, n_kv, nt, hd, qw = q_t.shape
    tt = k.shape[2]
    nq = nt - tile_off
    kern = functools.partial(_attn_kernel, hd=hd, tile_off=tile_off)
    return pl.pallas_call(
        kern,
        grid=(bsz, n_kv, nq),
        in_specs=[pl.BlockSpec((1, 1, 1, hd, qw), lambda b, h, i: (b, h, i + tile_off, 0, 0)),
                  pl.BlockSpec((1, 1, tt, hd), lambda b, h, i: (b, h, 0, 0)),
                  pl.BlockSpec((1, 1, hd, tt), lambda b, h, i: (b, h, 0, 0))],
        out_specs=pl.BlockSpec((1, TM, Q_PER_KV * hd), lambda b, h, i: (b, i, h)),
        out_shape=jax.ShapeDtypeStruct((bsz, nq * TM, n_kv * Q_PER_KV * hd), BF16),
        compiler_params=_cparams(3),
        name="gqa_attention",
    )(q_t, k, v_t)


def _odd_out_kernel(u_ref, up_ref, un_ref, pw_ref, ps_ref, at_ref, w_ref, x_ref, gate_ref, pg_ref, o_ref,
                    *, n_tiles_total, tile_off, pool_group, seq_latent):
    i = pl.program_id(1)
    j = i + tile_off
    pv, nv = _segment_halo_valid(j, n_tiles_total, True)
    u = u_ref[0]
    uext = jnp.concatenate([up_ref[0] * pv, u, un_ref[0] * nv], axis=0)
    n_ext = uext.shape[0]
    s2 = uext[:n_ext - 1] + uext[1:]
    s4 = s2[:n_ext - 3] + s2[2:]
    s8 = s4[:n_ext - 7] + s4[4:]
    s16 = s8[:n_ext - 15] + s8[8:]
    sums = (s2, s4, s8, s16)
    seg_start = jnp.where(j == 0, 0, 1)
    seg_len = jnp.where(j == 0, TM, seq_latent).astype(F32)
    t = ((j - seg_start) * TM + lax.broadcasted_iota(jnp.int32, (TM, 1), 0)).astype(F32)
    lane_group = lax.broadcasted_iota(jnp.int32, (1, u.shape[1]), 1) // pool_group
    mean = jnp.zeros_like(u)
    for gi, w in enumerate(POOL_WINDOWS):
        hw = w // 2
        win = sums[gi][SMALL_HALO - hw:SMALL_HALO - hw + TM]
        cnt = jnp.minimum(t + hw, seg_len) - jnp.maximum(t - hw, 0.0)
        mean = jnp.where(lane_group == gi, win / cnt, mean)
    dpool = (mean - u).astype(BF16)
    y = jnp.dot(dpool, pw_ref[...], preferred_element_type=F32) * ps_ref[...]
    cat = jnp.concatenate([y.astype(BF16), at_ref[0]], axis=1)
    z = jnp.dot(cat, w_ref[...], preferred_element_type=F32)
    o_ref[0] = x_ref[0] + gate_ref[0] * _rms(z, pg_ref[...])


def _odd_out(pool_u, attn, xx, gate, post_g, pool_bd, pool_scale, w_out, *, n_batch, tile_off, seq_latent):
    bsz, tt, d = xx.shape
    nt = tt // TM
    n_out = nt - tile_off
    dp = pool_u.shape[-1]
    hb = TM // SMALL_HALO
    n_hblk = tt // SMALL_HALO
    kern = functools.partial(_odd_out_kernel, n_tiles_total=nt, tile_off=tile_off,
                             pool_group=dp // len(POOL_WINDOWS), seq_latent=seq_latent)
    return pl.pallas_call(
        kern,
        grid=(bsz, n_out),
        in_specs=[pl.BlockSpec((1, TM, dp), lambda b, i: (b, i + tile_off, 0)),
                  pl.BlockSpec((1, SMALL_HALO, dp), lambda b, i: (b, jnp.maximum((i + tile_off) * hb - 1, 0), 0)),
                  pl.BlockSpec((1, SMALL_HALO, dp),
                               lambda b, i: (b, jnp.minimum((i + tile_off + 1) * hb, n_hblk - 1), 0)),
                  _const_spec(pool_bd.shape), _const_spec((1, dp)),
                  pl.BlockSpec((1, TM, attn.shape[-1]), lambda b, i: (b, i, 0)),
                  _const_spec(w_out.shape),
                  pl.BlockSpec((1, TM, d), lambda b, i: (b, i + tile_off, 0)),
                  _mod_spec(d, tile_off, n_batch), _const_spec((1, d))],
        out_specs=pl.BlockSpec((1, TM, d), lambda b, i: (b, i, 0)),
        out_shape=jax.ShapeDtypeStruct((bsz, n_out * TM, d), F32),
        compiler_params=_cparams(2),
        name="odd_out",
    )(pool_u, pool_u, pool_u, pool_bd, pool_scale, attn, w_out, xx, gate, post_g)


def _ffn_kernel(x_ref, xp_ref, xn_ref, sh_ref, sc_ref, gate_ref, g_ref, pg_ref, wi_ref, cw_ref, wo_ref, o_ref,
                *, n_tiles_total, has_ctx, n_chunks):
    i = pl.program_id(1)
    pv, nv = _segment_halo_valid(i, n_tiles_total, has_ctx)
    xc = x_ref[0]
    xe = jnp.concatenate([xp_ref[0], xc, xn_ref[0]], axis=0)
    he = _rms(xe, g_ref[...]) * (1.0 + sc_ref[0]) + sh_ref[0]
    row = lax.broadcasted_iota(jnp.int32, (xe.shape[0], 1), 0)
    keep = jnp.where(row < SMALL_HALO, pv, jnp.where(row >= SMALL_HALO + TM, nv, 1.0))
    he = (he * keep).astype(BF16)
    hc = he[SMALL_HALO:SMALL_HALO + TM]
    def up_proj(c):
        return (jnp.dot(he, wi_ref[c], preferred_element_type=F32),
                jnp.dot(hc, wi_ref[n_chunks + c], preferred_element_type=F32))

    acc = jnp.zeros((TM, o_ref.shape[2]), F32)
    nxt = up_proj(0)
    for c in range(n_chunks):
        g, v = nxt
        if c + 1 < n_chunks:
            nxt = up_proj(c + 1)
        cw = cw_ref[c]
        gc = (cw[0:1] * g[SMALL_HALO - 1:SMALL_HALO - 1 + TM] + cw[1:2] * g[SMALL_HALO:SMALL_HALO + TM]
              + cw[2:3] * g[SMALL_HALO + 1:SMALL_HALO + 1 + TM])
        u = (gc * jax.nn.sigmoid(gc) * v).astype(BF16)
        acc = acc + jnp.dot(u, wo_ref[c], preferred_element_type=F32)
    o_ref[0] = xc + gate_ref[0] * _rms(acc, pg_ref[...])


def _ffn(xx, shift, scale, gate, pre_g, post_g, w_in_c, conv_w_c, w_out_c, *, n_batch, has_ctx):
    bsz, tt, d = xx.shape
    nt = tt // TM
    hb = TM // SMALL_HALO
    n_hblk = tt // SMALL_HALO
    n_chunks = w_out_c.shape[0]
    off = 0 if has_ctx else 1
    kern = functools.partial(_ffn_kernel, n_tiles_total=nt, has_ctx=has_ctx, n_chunks=n_chunks)
    return pl.pallas_call(
        kern,
        grid=(bsz, nt),
        in_specs=[pl.BlockSpec((1, TM, d), lambda b, i: (b, i, 0)),
                  pl.BlockSpec((1, SMALL_HALO, d), lambda b, i: (b, jnp.maximum(i * hb - 1, 0), 0)),
                  pl.BlockSpec((1, SMALL_HALO, d), lambda b, i: (b, jnp.minimum((i + 1) * hb, n_hblk - 1), 0)),
                  _mod_spec(d, off, n_batch), _mod_spec(d, off, n_batch), _mod_spec(d, off, n_batch),
                  _const_spec((1, d)), _const_spec((1, d)),
                  _const_spec(w_in_c.shape), _const_spec(conv_w_c.shape), _const_spec(w_out_c.shape)],
        out_specs=pl.BlockSpec((1, TM, d), lambda b, i: (b, i, 0)),
        out_shape=jax.ShapeDtypeStruct((bsz, tt, d), F32),
        compiler_params=_cparams(2),
        name="conv_ffn",
    )(xx, xx, xx, shift, scale, gate, pre_g, post_g, w_in_c, conv_w_c, w_out_c)


def _rope_tables_t(seq, ctx_len, hd):
    rows = seq // GRID_W
    row = jnp.repeat(jnp.arange(rows), GRID_W).astype(F32)
    col = jnp.tile(jnp.arange(GRID_W), rows).astype(F32)
    n_freq = hd // 4
    inv = ROPE_THETA ** (-jnp.arange(n_freq, dtype=F32) / n_freq)
    ang = jnp.concatenate([row[:, None] * inv, col[:, None] * inv], axis=-1)
    cos = jnp.concatenate([jnp.ones((ctx_len, hd // 2), F32), jnp.cos(ang)], axis=0)
    sin = jnp.concatenate([jnp.zeros((ctx_len, hd // 2), F32), jnp.sin(ang)], axis=0)
    return cos.T, sin.T


def kernel(x, c, ctx, c_ctx, ada_w, ada_b, mix_pre_g, mix_post_g, ffn_pre_g, ffn_post_g, ffn_w_in, ffn_conv_w,
           ffn_w_out, even_w_in, even_w_out, conv_dw_w, conv_ln_g, conv_ln_b, mlstm_gate_b, mlstm_norm_g,
           odd_w_in, odd_w_out, pool_w, pool_scale, q_norm_g, k_norm_g):
    bsz, seq, d = x.shape
    ctx_len = ctx.shape[1]
    depth = ada_w.shape[0]
    assert ctx_len == TM and seq % TM == 0 and seq % GRID_W == 0
    dc = conv_dw_w.shape[2]
    dm = mlstm_norm_g.shape[1]
    n_heads = mlstm_gate_b.shape[2]
    hd_m = dm // n_heads
    assert n_heads % 2 == 0 and even_w_in.shape[2] == 2 * dc + 4 * dm + 4 * n_heads
    hd = q_norm_g.shape[1]
    dp = pool_scale.shape[1]
    n_kv = (odd_w_in.shape[2] - dp - (odd_w_out.shape[1] - dp)) // (2 * hd)
    n_q = (odd_w_out.shape[1] - dp) // hd
    assert n_q == n_kv * Q_PER_KV
    d_ff = ffn_w_out.shape[1]
    n_ff = d_ff // FF_CHUNK
    assert d_ff % FF_CHUNK == 0

    pad = (-(bsz + 1)) % 8
    cc = jnp.concatenate([c, c_ctx[None, :], jnp.zeros((pad, d), F32)], axis=0)
    mods = _ada_all(cc, ada_w, ada_b)

    def mod(l, k):
        return mods[l, :bsz + 1, k * d:(k + 1) * d].reshape(bsz + 1, 1, d)

    cos_t, sin_t = _rope_tables_t(seq, ctx_len, hd)
    perm = np.concatenate([np.arange(0, hd, 2), np.arange(1, hd, 2)])

    xx = jnp.concatenate([ctx, x], axis=1)
    row2 = lambda a: a.reshape(1, -1)

    for l in range(depth):
        last = l == depth - 1
        if l % 2 == 0:
            e = l // 2
            w_in = even_w_in[e]
            n_main = 2 * dc + 4 * dm
            w_main = w_in[:, :n_main].astype(BF16)
            wg = w_in[:, n_main:].reshape(d, 4, n_heads // 2, 2)
            w_gate_t = wg.transpose(2, 1, 3, 0).reshape(4 * n_heads, d).astype(BF16)
            gate_b_col = mlstm_gate_b[e].reshape(4, n_heads // 2, 2).transpose(1, 0, 2).reshape(4 * n_heads, 1)
            u, q, k, v, og, gt = _even_proj(xx, mod(l, 0), mod(l, 1), row2(mix_pre_g[l]), w_main, w_gate_t,
                                            gate_b_col, n_batch=bsz, dc=dc, dm=dm, hd=hd_m)
            mm = _mlstm(q, k, v, og, gt, row2(mlstm_norm_g[e]), hd=hd_m, n_ctx_chunks=ctx_len // MLSTM_CHUNK)
            xx = _even_out(u, mm, xx, mod(l, 2), row2(mix_post_g[l]), conv_dw_w[e], row2(conv_ln_g[e]),
                           row2(conv_ln_b[e]), even_w_out[e].astype(BF16), n_batch=bsz)
        else:
            o = l // 2
            w_in = odd_w_in[o]
            w_pool = w_in[:, :dp].astype(BF16)
            w_qk = w_in[:, dp:dp + (n_q + n_kv) * hd].reshape(d, n_q + n_kv, hd)[:, :, perm]
            w_qk_t = w_qk.reshape(d, -1).T.astype(BF16)
            w_v_t = w_in[:, dp + (n_q + n_kv) * hd:].T.astype(BF16)
            pool_u, q_t, k, v_t = _odd_proj(xx, mod(l, 0), mod(l, 1), row2(mix_pre_g[l]), w_pool, w_qk_t, w_v_t,
                                            cos_t, sin_t, q_norm_g[o][perm].reshape(hd, 1),
                                            k_norm_g[o][perm].reshape(hd, 1), n_batch=bsz, hd=hd, n_q=n_q, n_kv=n_kv)
            tile_off = 1 if last else 0
            attn = _attention(q_t, k, v_t, tile_off=tile_off)
            pg = dp // len(POOL_WINDOWS)
            pool_bd = jnp.zeros((dp, dp), F32)
            for gi in range(len(POOL_WINDOWS)):
                pool_bd = pool_bd.at[gi * pg:(gi + 1) * pg, gi * pg:(gi + 1) * pg].set(pool_w[o, gi])
            xx = _odd_out(pool_u, attn, xx, mod(l, 2), row2(mix_post_g[l]), pool_bd.astype(BF16),
                          row2(pool_scale[o]), odd_w_out[o].astype(BF16), n_batch=bsz, tile_off=tile_off,
                          seq_latent=seq)
        w_in_f = ffn_w_in[l].astype(BF16).reshape(d, 2 * n_ff, FF_CHUNK).transpose(1, 0, 2)
        conv_c = ffn_conv_w[l].reshape(-1, n_ff, FF_CHUNK).transpose(1, 0, 2)
        w_out_f = ffn_w_out[l].astype(BF16).reshape(n_ff, FF_CHUNK, d)
        xx = _ffn(xx, mod(l, 3), mod(l, 4), mod(l, 5), row2(ffn_pre_g[l]), row2(ffn_post_g[l]),
                  w_in_f, conv_c, w_out_f, n_batch=bsz, has_ctx=xx.shape[1] != seq)
    return xx if xx.shape[1] == seq else xx[:, ctx_len:]
```

```python
import functools

import jax
import jax.numpy as jnp
import numpy as np
from jax import lax
from jax.experimental import pallas as pl
from jax.experimental.pallas import tpu as pltpu

F32 = jnp.float32
BF16 = jnp.bfloat16

GRID_W = 64
MLSTM_CHUNK = 64
POOL_WINDOWS = (2, 4, 8, 16)
Q_PER_KV = 3
ROPE_THETA = 10000.0
EPS = 1e-6
LOG2_E = 1.4426950408889634

TM = 256
CONV_HALO = 16
SMALL_HALO = 8
FF_CHUNK = 256
ATTN_KEY_CHUNK = 256
VMEM_LIMIT = 56 * 1024 * 1024


def _cparams(n_axes, vmem=VMEM_LIMIT):
    return pltpu.CompilerParams(dimension_semantics=("arbitrary",) * n_axes, vmem_limit_bytes=vmem)


def _const_spec(shape):
    nd = len(shape)
    return pl.BlockSpec(shape, lambda *_: (0,) * nd)


def _rms(x, g):
    return x * lax.rsqrt(jnp.mean(x * x, axis=-1, keepdims=True) + EPS) * g


def _segment_halo_valid(j, n_tiles_total, has_ctx):
    if has_ctx:
        prev_ok = jnp.logical_and(j != 0, j != 1)
        next_ok = jnp.logical_and(j != 0, j != n_tiles_total - 1)
    else:
        prev_ok = j != 0
        next_ok = j != n_tiles_total - 1
    return prev_ok.astype(F32), next_ok.astype(F32)


def _ada_kernel(c_ref, w_ref, b_ref, o_ref):
    c = c_ref[...]
    s = c * jax.nn.sigmoid(c)
    o_ref[0] = jnp.dot(s, w_ref[0], preferred_element_type=F32, precision=lax.Precision.HIGHEST) + b_ref[0]


def _ada_all(cc, ada_w, ada_b):
    depth, d, n = ada_w.shape
    nb = n // 4
    rows = cc.shape[0]
    return pl.pallas_call(
        _ada_kernel,
        grid=(depth, n // nb),
        in_specs=[pl.BlockSpec((rows, d), lambda l, j: (0, 0)),
                  pl.BlockSpec((1, d, nb), lambda l, j: (l, 0, j)),
                  pl.BlockSpec((1, 1, nb), lambda l, j: (l, 0, j))],
        out_specs=pl.BlockSpec((1, rows, nb), lambda l, j: (l, 0, j)),
        out_shape=jax.ShapeDtypeStruct((depth, rows, n), F32),
        compiler_params=_cparams(2),
        name="ada_mod",
    )(cc, ada_w, ada_b.reshape(depth, 1, n))


def _mod_spec(d, off, n_batch):
    if off == 0:
        return pl.BlockSpec((1, 1, d), lambda b, i: (jnp.where(i == 0, n_batch, b), 0, 0))
    return pl.BlockSpec((1, 1, d), lambda b, i: (b, 0, 0))


def _even_proj_kernel(x_ref, sh_ref, sc_ref, g_ref, w_ref, wg_ref, gb_ref,
                      u_ref, q_ref, k_ref, v_ref, o_ref, gt_ref, *, dc, dm, k_scale, n_sub):
    x = x_ref[0]
    h = (_rms(x, g_ref[...]) * (1.0 + sc_ref[0]) + sh_ref[0]).astype(BF16)
    p = jnp.dot(h, w_ref[...], preferred_element_type=F32)
    u_ref[0] = (p[:, :dc] * jax.nn.sigmoid(p[:, dc:2 * dc])).astype(BF16)
    off = 2 * dc
    q_ref[0] = p[:, off:off + dm].astype(BF16)
    k_ref[0] = (p[:, off + dm:off + 2 * dm] * k_scale).astype(BF16)
    v_ref[0] = p[:, off + 2 * dm:off + 3 * dm].astype(BF16)
    o_ref[0] = jax.nn.sigmoid(p[:, off + 3 * dm:off + 4 * dm]).astype(BF16)
    gt = lax.dot_general(wg_ref[...], h, (((1,), (1,)), ((), ())), preferred_element_type=F32) + gb_ref[...]
    row = lax.broadcasted_iota(jnp.int32, gt.shape, 0)
    gt = jnp.where((row & 2) != 0, jax.nn.log_sigmoid(gt), gt)
    n_pairs = gt.shape[0] // 8
    for pr in range(n_pairs):
        for c in range(n_sub):
            gt_ref[0, pr, c] = gt[pr * 8:(pr + 1) * 8, c * MLSTM_CHUNK:(c + 1) * MLSTM_CHUNK]


def _even_proj(xx, shift, scale, pre_g, w_main, w_gate_t, gate_b_col, *, n_batch, dc, dm, hd):
    bsz, tt, d = xx.shape
    nt = tt // TM
    n_sub = TM // MLSTM_CHUNK
    n_pairs = w_gate_t.shape[0] // 8
    nc = tt // MLSTM_CHUNK
    tok = lambda w: pl.BlockSpec((1, TM, w), lambda b, i: (b, i, 0))
    kern = functools.partial(_even_proj_kernel, dc=dc, dm=dm, k_scale=float(hd) ** -0.5, n_sub=n_sub)
    return pl.pallas_call(
        kern,
        grid=(bsz, nt),
        in_specs=[tok(d), _mod_spec(d, 0, n_batch), _mod_spec(d, 0, n_batch), _const_spec((1, d)),
                  _const_spec(w_main.shape), _const_spec(w_gate_t.shape), _const_spec(gate_b_col.shape)],
        out_specs=[tok(dc), tok(dm), tok(dm), tok(dm), tok(dm),
                   pl.BlockSpec((1, n_pairs, n_sub, 8, MLSTM_CHUNK), lambda b, i: (b, 0, i, 0, 0))],
        out_shape=[jax.ShapeDtypeStruct((bsz, tt, dc), BF16)] + [jax.ShapeDtypeStruct((bsz, tt, dm), BF16)] * 4
        + [jax.ShapeDtypeStruct((bsz, n_pairs, nc, 8, MLSTM_CHUNK), F32)],
        compiler_params=_cparams(2),
        name="even_proj",
    )(xx, shift, scale, pre_g, w_main, w_gate_t, gate_b_col)


def _mlstm_kernel(q_ref, k_ref, v_ref, og_ref, gt_ref, ng_ref, out_ref,
                  hf_ref, hb_ref, c_ref, grow_ref, gmax_ref, blast_ref, mprev_ref,
                  *, hd, n_chunks, n_ctx_chunks, tt):
    L = MLSTM_CHUNK
    rowi = lax.broadcasted_iota(jnp.int32, (L, L), 0)
    coli = lax.broadcasted_iota(jnp.int32, (L, L), 1)
    eye = rowi == coli
    tri_by_dir = (coli <= rowi, coli >= rowi)
    ones_blk = jnp.ones((L, hd), BF16)

    def chunk_of(d, s):
        if d == 0:
            return s
        return jnp.where(s < n_ctx_chunks, n_ctx_chunks - 1 - s, n_chunks - 1 - (s - n_ctx_chunks))

    for d in range(2):
        cum = jnp.where(rowi <= coli if d == 0 else rowi >= coli, 1.0, 0.0)
        for hh in range(2):
            ci = d * 2 + hh
            i_rows = gt_ref[0, 0, :, 4 * d + hh, :]
            lf_rows = gt_ref[0, 0, :, 4 * d + 2 + hh, :]
            b_rows = jnp.dot(lf_rows, cum, preferred_element_type=F32, precision=lax.Precision.HIGHEST)
            g_rows = i_rows - b_rows
            grow_ref[ci, :n_chunks, :] = g_rows
            gmax_ref[ci, :n_chunks, :] = jnp.broadcast_to(jnp.max(g_rows, axis=1, keepdims=True), (n_chunks, 128))
            blast_ref[ci, :n_chunks, :] = jnp.broadcast_to(jnp.sum(lf_rows, axis=1, keepdims=True), (n_chunks, 128))

    def scan(s, ms):
        out = []
        for d in range(2):
            c = chunk_of(d, s)
            for hh in range(2):
                ci = d * 2 + hh
                mprev_ref[ci, pl.ds(c, 1), :] = ms[ci]
                out.append(blast_ref[ci, pl.ds(c, 1), :] + jnp.maximum(ms[ci], gmax_ref[ci, pl.ds(c, 1), :]))
        return tuple(out)

    lax.fori_loop(0, n_chunks, scan, (jnp.zeros((1, 128), F32),) * 4)

    c_ref[...] = jnp.zeros(c_ref.shape, F32)

    def step(s, carry):
        chains = []
        for d in range(2):
            c = chunk_of(d, s)
            r0 = pl.multiple_of(c * L, L)
            gt = gt_ref[0, 0, c]
            tri = tri_by_dir[d]
            for hh in range(2):
                ci = d * 2 + hh
                cols = slice(hh * hd, (hh + 1) * hd)
                lf_row = gt[4 * d + 2 + hh:4 * d + 3 + hh, :]
                g_row = grow_ref[ci, pl.ds(c, 1), :]
                m_prev = mprev_ref[ci, pl.ds(c, 1), :][:, :1]
                g_max = gmax_ref[ci, pl.ds(c, 1), :][:, :1]
                b_col = jnp.sum(jnp.where(tri, lf_row, 0.0), axis=1, keepdims=True)
                g_col = jnp.sum(jnp.where(eye, g_row, 0.0), axis=1, keepdims=True)
                cg_col = jnp.max(jnp.where(tri, g_row, -jnp.inf), axis=1, keepdims=True)
                m_col = jnp.maximum(m_prev, cg_col)
                g_top = jnp.maximum(m_prev, g_max)
                qh = q_ref[0, pl.ds(r0, L), cols]
                kh = k_ref[0, pl.ds(r0, L), cols]
                v_aug = jnp.concatenate([v_ref[0, pl.ds(r0, L), cols], ones_blk], axis=1)
                chains.append(dict(
                    ci=ci, r0=r0, cols=cols, h_ref=hf_ref if d == 0 else hb_ref, qh=qh, kh=kh, v_aug=v_aug,
                    decay=jnp.where(tri, jnp.exp(g_row - m_col), 0.0), w_col=jnp.exp(m_prev - m_col),
                    clamp=jnp.exp(-(b_col + m_col)), keep=jnp.exp(m_prev - g_top),
                    ev=(jnp.exp(g_col - g_top) * v_aug.astype(F32)).astype(BF16)))
        for ch in chains:
            ch["qk"] = lax.dot_general(ch["qh"], ch["kh"], (((1,), (1,)), ((), ())), preferred_element_type=F32)
        for ch in chains:
            ch["c_aug"] = c_ref[ch["ci"]]
            ch["qc"] = jnp.dot(ch["qh"], ch["c_aug"].astype(BF16), preferred_element_type=F32)
        for ch in chains:
            ch["upd"] = lax.dot_general(ch["kh"], ch["ev"], (((0,), (0,)), ((), ())), preferred_element_type=F32)
        for ch in chains:
            ch["sv"] = jnp.dot((ch["qk"] * ch["decay"]).astype(BF16), ch["v_aug"], preferred_element_type=F32)
        for ch in chains:
            tot = ch["w_col"] * ch["qc"] + ch["sv"]
            ch["h_ref"][pl.ds(ch["r0"], L), ch["cols"]] = tot[:, :hd] / jnp.maximum(jnp.abs(tot[:, hd:]), ch["clamp"])
            c_ref[ch["ci"]] = ch["keep"] * ch["c_aug"] + ch["upd"]
        return carry

    lax.fori_loop(0, n_chunks, step, 0, unroll=2)

    def merge(t, carry):
        r0 = pl.multiple_of(t * TM, TM)
        hsum = hf_ref[pl.ds(r0, TM), :] + hb_ref[pl.ds(r0, TM), :]
        og = og_ref[0, pl.ds(r0, TM), :].astype(F32)
        ng = ng_ref[...]
        parts = []
        for hh in range(2):
            cols = slice(hh * hd, (hh + 1) * hd)
            parts.append(_rms(hsum[:, cols], ng[:, cols]))
        out_ref[0, pl.ds(r0, TM), :] = (jnp.concatenate(parts, axis=1) * og).astype(BF16)
        return carry

    lax.fori_loop(0, tt // TM, merge, 0)


def _mlstm(q, k, v, og, gt, norm_g, *, hd, n_ctx_chunks):
    bsz, tt, dm = q.shape
    n_pairs = gt.shape[1]
    nc = gt.shape[2]
    pw = 2 * hd
    nc_pad = -(-nc // 8) * 8
    tok = pl.BlockSpec((1, tt, pw), lambda b, p: (b, 0, p))
    kern = functools.partial(_mlstm_kernel, hd=hd, n_chunks=nc, n_ctx_chunks=n_ctx_chunks, tt=tt)
    return pl.pallas_call(
        kern,
        grid=(bsz, n_pairs),
        in_specs=[tok, tok, tok, tok,
                  pl.BlockSpec((1, 1, nc, 8, MLSTM_CHUNK), lambda b, p: (b, p, 0, 0, 0)),
                  pl.BlockSpec((1, pw), lambda b, p: (0, p))],
        out_specs=tok,
        out_shape=jax.ShapeDtypeStruct((bsz, tt, dm), BF16),
        scratch_shapes=[pltpu.VMEM((tt, pw), F32), pltpu.VMEM((tt, pw), F32),
                        pltpu.VMEM((4, hd, 2 * hd), F32), pltpu.VMEM((4, nc_pad, MLSTM_CHUNK), F32),
                        pltpu.VMEM((4, nc_pad, 128), F32), pltpu.VMEM((4, nc_pad, 128), F32),
                        pltpu.VMEM((4, nc_pad, 128), F32)],
        compiler_params=_cparams(2),
        name="mlstm",
    )(q, k, v, og, gt, norm_g)


def _even_out_kernel(u_ref, up_ref, un_ref, dw_ref, lg_ref, lb_ref, mm_ref, w_ref, x_ref, gate_ref, pg_ref,
                     o_ref, sh_ref, *, n_tiles_total, width):
    i = pl.program_id(1)
    pv, nv = _segment_halo_valid(i, n_tiles_total, True)
    uext = jnp.concatenate([up_ref[0].astype(F32) * pv, u_ref[0].astype(F32), un_ref[0].astype(F32) * nv], axis=0)
    dw = dw_ref[...]
    base = CONV_HALO - width // 2
    span = sh_ref.shape[1]
    for r in range(8):
        sh_ref[r] = uext[base + r:base + r + span]
    acc = jnp.zeros((TM, uext.shape[1]), F32)
    for j in range(width):
        acc = acc + dw[j:j + 1, :] * sh_ref[j % 8, 8 * (j // 8):8 * (j // 8) + TM, :]
    mu = jnp.mean(acc, axis=-1, keepdims=True)
    cen = acc - mu
    var = jnp.mean(cen * cen, axis=-1, keepdims=True)
    y = cen * lax.rsqrt(var + EPS) * lg_ref[...] + lb_ref[...]
    conv_out = (y * jax.nn.sigmoid(y)).astype(BF16)
    cat = jnp.concatenate([conv_out, mm_ref[0]], axis=1)
    z = jnp.dot(cat, w_ref[...], preferred_element_type=F32)
    o_ref[0] = x_ref[0] + gate_ref[0] * _rms(z, pg_ref[...])


def _even_out(u, mm, xx, gate, post_g, dw_w, ln_g, ln_b, w_out, *, n_batch):
    bsz, tt, d = xx.shape
    nt = tt // TM
    dc = u.shape[-1]
    hb = TM // CONV_HALO
    n_hblk = tt // CONV_HALO
    tok = lambda w: pl.BlockSpec((1, TM, w), lambda b, i: (b, i, 0))
    kern = functools.partial(_even_out_kernel, n_tiles_total=nt, width=dw_w.shape[0])
    return pl.pallas_call(
        kern,
        grid=(bsz, nt),
        in_specs=[tok(dc),
                  pl.BlockSpec((1, CONV_HALO, dc), lambda b, i: (b, jnp.maximum(i * hb - 1, 0), 0)),
                  pl.BlockSpec((1, CONV_HALO, dc), lambda b, i: (b, jnp.minimum((i + 1) * hb, n_hblk - 1), 0)),
                  _const_spec(dw_w.shape), _const_spec((1, dc)), _const_spec((1, dc)),
                  tok(mm.shape[-1]), _const_spec(w_out.shape), tok(d), _mod_spec(d, 0, n_batch),
                  _const_spec((1, d))],
        out_specs=tok(d),
        out_shape=jax.ShapeDtypeStruct((bsz, tt, d), F32),
        scratch_shapes=[pltpu.VMEM((8, TM + 8 * ((dw_w.shape[0] - 1) // 8), dc), F32)],
        compiler_params=_cparams(2),
        name="even_out",
    )(u, u, u, dw_w, ln_g, ln_b, mm, w_out, xx, gate, post_g)


def _odd_proj_kernel(x_ref, sh_ref, sc_ref, g_ref, wp_ref, wqk_ref, wv_ref, cos_ref, sin_ref, qg_ref, kg_ref,
                     pool_ref, qt_ref, k_ref, vt_ref, *, hd, n_q, n_kv, q_scale):
    x = x_ref[0]
    h = (_rms(x, g_ref[...]) * (1.0 + sc_ref[0]) + sh_ref[0]).astype(BF16)
    pool_ref[0] = jnp.dot(h, wp_ref[...], preferred_element_type=F32)
    nt_dims = (((1,), (1,)), ((), ()))
    qk_t = lax.dot_general(wqk_ref[...], h, nt_dims, preferred_element_type=F32)
    cos = cos_ref[...]
    sin = sin_ref[...]
    half = hd // 2
    for hi in range(n_q + n_kv):
        t = qk_t[hi * hd:(hi + 1) * hd]
        gain = qg_ref[...] if hi < n_q else kg_ref[...]
        tn = t * lax.rsqrt(jnp.mean(t * t, axis=0, keepdims=True) + EPS) * gain
        x1 = tn[:half]
        x2 = tn[half:]
        rot = jnp.concatenate([x1 * cos - x2 * sin, x1 * sin + x2 * cos], axis=0)
        if hi < n_q:
            kvh, gq = divmod(hi, Q_PER_KV)
            qt_ref[0, kvh, 0, :, gq * TM:(gq + 1) * TM] = (rot * q_scale).astype(BF16)
        else:
            k_ref[0, hi - n_q] = rot.T.astype(BF16)
    v_t = lax.dot_general(wv_ref[...], h, nt_dims, preferred_element_type=F32)
    for kvh in range(n_kv):
        vt_ref[0, kvh] = v_t[kvh * hd:(kvh + 1) * hd].astype(BF16)


def _odd_proj(xx, shift, scale, pre_g, w_pool, w_qk_t, w_v_t, cos_t, sin_t, qg_col, kg_col, *, n_batch, hd, n_q, n_kv):
    bsz, tt, d = xx.shape
    nt = tt // TM
    dp = w_pool.shape[1]
    kern = functools.partial(_odd_proj_kernel, hd=hd, n_q=n_q, n_kv=n_kv, q_scale=float(hd) ** -0.5 * LOG2_E)
    return pl.pallas_call(
        kern,
        grid=(bsz, nt),
        in_specs=[pl.BlockSpec((1, TM, d), lambda b, i: (b, i, 0)),
                  _mod_spec(d, 0, n_batch), _mod_spec(d, 0, n_batch), _const_spec((1, d)),
                  _const_spec(w_pool.shape), _const_spec(w_qk_t.shape), _const_spec(w_v_t.shape),
                  pl.BlockSpec((hd // 2, TM), lambda b, i: (0, i)), pl.BlockSpec((hd // 2, TM), lambda b, i: (0, i)),
                  _const_spec((hd, 1)), _const_spec((hd, 1))],
        out_specs=[pl.BlockSpec((1, TM, dp), lambda b, i: (b, i, 0)),
                   pl.BlockSpec((1, n_kv, 1, hd, Q_PER_KV * TM), lambda b, i: (b, 0, i, 0, 0)),
                   pl.BlockSpec((1, n_kv, TM, hd), lambda b, i: (b, 0, i, 0)),
                   pl.BlockSpec((1, n_kv, hd, TM), lambda b, i: (b, 0, 0, i))],
        out_shape=[jax.ShapeDtypeStruct((bsz, tt, dp), F32),
                   jax.ShapeDtypeStruct((bsz, n_kv, nt, hd, Q_PER_KV * TM), BF16),
                   jax.ShapeDtypeStruct((bsz, n_kv, tt, hd), BF16),
                   jax.ShapeDtypeStruct((bsz, n_kv, hd, tt), BF16)],
        compiler_params=_cparams(2),
        name="odd_proj",
    )(xx, shift, scale, pre_g, w_pool, w_qk_t, w_v_t, cos_t, sin_t, qg_col, kg_col)


def _attn_kernel(qt_ref, k_ref, vt_ref, o_ref, *, hd, tile_off):
    n_kv = k_ref.shape[1]
    heads = [(kvh, g) for kvh in range(n_kv) for g in range(Q_PER_KV)]

    def attend(n_keys):
        kc = ATTN_KEY_CHUNK
        n_kc = n_keys // kc

        def fold8(a, op):
            return op(a.reshape(kc // 8, 8, TM), axis=0)

        def scores(kvh, g):
            q_t = qt_ref[0, kvh, 0, :, g * TM:(g + 1) * TM]
            chunks, mx = [], None
            for j in range(n_kc):
                s = jnp.dot(k_ref[0, kvh, j * kc:(j + 1) * kc, :], q_t, preferred_element_type=F32)
                chunks.append(s)
                m8 = fold8(s, jnp.max)
                mx = m8 if mx is None else jnp.maximum(mx, m8)
            return chunks, jnp.max(mx, axis=0, keepdims=True)

        nxt = scores(*heads[0])
        for n, (kvh, g) in enumerate(heads):
            chunks, mx = nxt
            if n + 1 < len(heads):
                nxt = scores(*heads[n + 1])
            den8, probs = None, []
            for s in chunks:
                p = jnp.exp2(s - mx)
                d8 = fold8(p, jnp.sum)
                den8 = d8 if den8 is None else den8 + d8
                probs.append(p.astype(BF16))
            den = jnp.sum(den8, axis=0, keepdims=True)
            o_t = jnp.dot(vt_ref[0, kvh, :, :n_keys], jnp.concatenate(probs, axis=0),
                          preferred_element_type=F32)
            col = (kvh * Q_PER_KV + g) * hd
            o_ref[0, :, col:col + hd] = (o_t / den).T.astype(BF16)

    if tile_off == 0:
        i = pl.program_id(1)

        @pl.when(i == 0)
        def _():
            attend(TM)

        @pl.when(i != 0)
        def _():
            attend(k_ref.shape[2])
    else:
        attend(k_ref.shape[2])


def _attention(q_t, k, v_t, *, tile_off):
    bsz, n_kv, nt, hd, qw = q_t.shape
    tt = k.shape[2]
    nq = nt - tile_off
    kern = functools.partial(_attn_kernel, hd=hd, tile_off=tile_off)
    return pl.pallas_call(
        kern,
        grid=(bsz, nq),
        in_specs=[pl.BlockSpec((1, n_kv, 1, hd, qw), lambda b, i: (b, 0, i + tile_off, 0, 0)),
                  pl.BlockSpec((1, n_kv, tt, hd), lambda b, i: (b, 0, 0, 0)),
                  pl.BlockSpec((1, n_kv, hd, tt), lambda b, i: (b, 0, 0, 0))],
        out_specs=pl.BlockSpec((1, TM, n_kv * Q_PER_KV * hd), lambda b, i: (b, i, 0)),
        out_shape=jax.ShapeDtypeStruct((bsz, nq * TM, n_kv * Q_PER_KV * hd), BF16),
        compiler_params=_cparams(2),
        name="gqa_attention",
    )(q_t, k, v_t)


def _odd_out_kernel(u_ref, up_ref, un_ref, pw_ref, ps_ref, at_ref, w_ref, x_ref, gate_ref, pg_ref, o_ref,
                    *, n_tiles_total, tile_off, pool_group, seq_latent):
    i = pl.program_id(1)
    j = i + tile_off
    pv, nv = _segment_halo_valid(j, n_tiles_total, True)
    u = u_ref[0]
    uext = jnp.concatenate([up_ref[0] * pv, u, un_ref[0] * nv], axis=0)
    n_ext = uext.shape[0]
    s2 = uext[:n_ext - 1] + uext[1:]
    s4 = s2[:n_ext - 3] + s2[2:]
    s8 = s4[:n_ext - 7] + s4[4:]
    s16 = s8[:n_ext - 15] + s8[8:]
    sums = (s2, s4, s8, s16)
    seg_start = jnp.where(j == 0, 0, 1)
    seg_len = jnp.where(j == 0, TM, seq_latent).astype(F32)
    t = ((j - seg_start) * TM + lax.broadcasted_iota(jnp.int32, (TM, 1), 0)).astype(F32)
    lane_group = lax.broadcasted_iota(jnp.int32, (1, u.shape[1]), 1) // pool_group
    mean = jnp.zeros_like(u)
    for gi, w in enumerate(POOL_WINDOWS):
        hw = w // 2
        win = sums[gi][SMALL_HALO - hw:SMALL_HALO - hw + TM]
        cnt = jnp.minimum(t + hw, seg_len) - jnp.maximum(t - hw, 0.0)
        mean = jnp.where(lane_group == gi, win / cnt, mean)
    dpool = (mean - u).astype(BF16)
    y = jnp.dot(dpool, pw_ref[...], preferred_element_type=F32) * ps_ref[...]
    cat = jnp.concatenate([y.astype(BF16), at_ref[0]], axis=1)
    z = jnp.dot(cat, w_ref[...], preferred_element_type=F32)
    o_ref[0] = x_ref[0] + gate_ref[0] * _rms(z, pg_ref[...])


def _odd_out(pool_u, attn, xx, gate, post_g, pool_bd, pool_scale, w_out, *, n_batch, tile_off, seq_latent):
    bsz, tt, d = xx.shape
    nt = tt // TM
    n_out = nt - tile_off
    dp = pool_u.shape[-1]
    hb = TM // SMALL_HALO
    n_hblk = tt // SMALL_HALO
    kern = functools.partial(_odd_out_kernel, n_tiles_total=nt, tile_off=tile_off,
                             pool_group=dp // len(POOL_WINDOWS), seq_latent=seq_latent)
    return pl.pallas_call(
        kern,
        grid=(bsz, n_out),
        in_specs=[pl.BlockSpec((1, TM, dp), lambda b, i: (b, i + tile_off, 0)),
                  pl.BlockSpec((1, SMALL_HALO, dp), lambda b, i: (b, jnp.maximum((i + tile_off) * hb - 1, 0), 0)),
                  pl.BlockSpec((1, SMALL_HALO, dp),
                               lambda b, i: (b, jnp.minimum((i + tile_off + 1) * hb, n_hblk - 1), 0)),
                  _const_spec(pool_bd.shape), _const_spec((1, dp)),
                  pl.BlockSpec((1, TM, attn.shape[-1]), lambda b, i: (b, i, 0)),
                  _const_spec(w_out.shape),
                  pl.BlockSpec((1, TM, d), lambda b, i: (b, i + tile_off, 0)),
                  _mod_spec(d, tile_off, n_batch), _const_spec((1, d))],
        out_specs=pl.BlockSpec((1, TM, d), lambda b, i: (b, i, 0)),
        out_shape=jax.ShapeDtypeStruct((bsz, n_out * TM, d), F32),
        compiler_params=_cparams(2),
        name="odd_out",
    )(pool_u, pool_u, pool_u, pool_bd, pool_scale, attn, w_out, xx, gate, post_g)


def _ffn_kernel(x_ref, xp_ref, xn_ref, sh_ref, sc_ref, gate_ref, g_ref, pg_ref, wi_ref, cw_ref, wo_ref, o_ref,
                *, n_tiles_total, has_ctx):
    d_ff = wo_ref.shape[0]
    n_chunks = d_ff // FF_CHUNK
    i = pl.program_id(1)
    pv, nv = _segment_halo_valid(i, n_tiles_total, has_ctx)
    xc = x_ref[0]
    xe = jnp.concatenate([xp_ref[0], xc, xn_ref[0]], axis=0)
    he = _rms(xe, g_ref[...]) * (1.0 + sc_ref[0]) + sh_ref[0]
    row = lax.broadcasted_iota(jnp.int32, (xe.shape[0], 1), 0)
    keep = jnp.where(row < SMALL_HALO, pv, jnp.where(row >= SMALL_HALO + TM, nv, 1.0))
    he = (he * keep).astype(BF16)
    hc = he[SMALL_HALO:SMALL_HALO + TM]
    def up_proj(c):
        lo = c * FF_CHUNK
        return (jnp.dot(he, wi_ref[:, lo:lo + FF_CHUNK], preferred_element_type=F32),
                jnp.dot(hc, wi_ref[:, d_ff + lo:d_ff + lo + FF_CHUNK], preferred_element_type=F32))

    acc = jnp.zeros((TM, o_ref.shape[2]), F32)
    nxt = up_proj(0)
    for c in range(n_chunks):
        g, v = nxt
        if c + 1 < n_chunks:
            nxt = up_proj(c + 1)
        cw = cw_ref[:, c * FF_CHUNK:(c + 1) * FF_CHUNK]
        gc = (cw[0:1] * g[SMALL_HALO - 1:SMALL_HALO - 1 + TM] + cw[1:2] * g[SMALL_HALO:SMALL_HALO + TM]
              + cw[2:3] * g[SMALL_HALO + 1:SMALL_HALO + 1 + TM])
        u = (gc * jax.nn.sigmoid(gc) * v).astype(BF16)
        acc = acc + jnp.dot(u, wo_ref[c * FF_CHUNK:(c + 1) * FF_CHUNK, :], preferred_element_type=F32)
    o_ref[0] = xc + gate_ref[0] * _rms(acc, pg_ref[...])


def _ffn(xx, shift, scale, gate, pre_g, post_g, w_in, conv_w, w_out, *, n_batch, has_ctx):
    bsz, tt, d = xx.shape
    nt = tt // TM
    hb = TM // SMALL_HALO
    n_hblk = tt // SMALL_HALO
    off = 0 if has_ctx else 1
    kern = functools.partial(_ffn_kernel, n_tiles_total=nt, has_ctx=has_ctx)
    return pl.pallas_call(
        kern,
        grid=(bsz, nt),
        in_specs=[pl.BlockSpec((1, TM, d), lambda b, i: (b, i, 0)),
                  pl.BlockSpec((1, SMALL_HALO, d), lambda b, i: (b, jnp.maximum(i * hb - 1, 0), 0)),
                  pl.BlockSpec((1, SMALL_HALO, d), lambda b, i: (b, jnp.minimum((i + 1) * hb, n_hblk - 1), 0)),
                  _mod_spec(d, off, n_batch), _mod_spec(d, off, n_batch), _mod_spec(d, off, n_batch),
                  _const_spec((1, d)), _const_spec((1, d)),
                  _const_spec(w_in.shape), _const_spec(conv_w.shape), _const_spec(w_out.shape)],
        out_specs=pl.BlockSpec((1, TM, d), lambda b, i: (b, i, 0)),
        out_shape=jax.ShapeDtypeStruct((bsz, tt, d), F32),
        compiler_params=_cparams(2),
        name="conv_ffn",
    )(xx, xx, xx, shift, scale, gate, pre_g, post_g, w_in, conv_w, w_out)


def _rope_tables_t(seq, ctx_len, hd):
    rows = seq // GRID_W
    row = jnp.repeat(jnp.arange(rows), GRID_W).astype(F32)
    col = jnp.tile(jnp.arange(GRID_W), rows).astype(F32)
    n_freq = hd // 4
    inv = ROPE_THETA ** (-jnp.arange(n_freq, dtype=F32) / n_freq)
    ang = jnp.concatenate([row[:, None] * inv, col[:, None] * inv], axis=-1)
    cos = jnp.concatenate([jnp.ones((ctx_len, hd // 2), F32), jnp.cos(ang)], axis=0)
    sin = jnp.concatenate([jnp.zeros((ctx_len, hd // 2), F32), jnp.sin(ang)], axis=0)
    return cos.T, sin.T


def kernel(x, c, ctx, c_ctx, ada_w, ada_b, mix_pre_g, mix_post_g, ffn_pre_g, ffn_post_g, ffn_w_in, ffn_conv_w,
           ffn_w_out, even_w_in, even_w_out, conv_dw_w, conv_ln_g, conv_ln_b, mlstm_gate_b, mlstm_norm_g,
           odd_w_in, odd_w_out, pool_w, pool_scale, q_norm_g, k_norm_g):
    bsz, seq, d = x.shape
    ctx_len = ctx.shape[1]
    depth = ada_w.shape[0]
    assert ctx_len == TM and seq % TM == 0 and seq % GRID_W == 0
    dc = conv_dw_w.shape[2]
    dm = mlstm_norm_g.shape[1]
    n_heads = mlstm_gate_b.shape[2]
    hd_m = dm // n_heads
    assert n_heads % 2 == 0 and even_w_in.shape[2] == 2 * dc + 4 * dm + 4 * n_heads
    hd = q_norm_g.shape[1]
    dp = pool_scale.shape[1]
    n_kv = (odd_w_in.shape[2] - dp - (odd_w_out.shape[1] - dp)) // (2 * hd)
    n_q = (odd_w_out.shape[1] - dp) // hd
    assert n_q == n_kv * Q_PER_KV
    assert ffn_w_out.shape[1] % FF_CHUNK == 0

    pad = (-(bsz + 1)) % 8
    cc = jnp.concatenate([c, c_ctx[None, :], jnp.zeros((pad, d), F32)], axis=0)
    mods = _ada_all(cc, ada_w, ada_b)

    def mod(l, k):
        return mods[l, :bsz + 1, k * d:(k + 1) * d].reshape(bsz + 1, 1, d)

    cos_t, sin_t = _rope_tables_t(seq, ctx_len, hd)
    perm = np.concatenate([np.arange(0, hd, 2), np.arange(1, hd, 2)])

    xx = jnp.concatenate([ctx, x], axis=1)
    row2 = lambda a: a.reshape(1, -1)

    for l in range(depth):
        last = l == depth - 1
        if l % 2 == 0:
            e = l // 2
            w_in = even_w_in[e]
            n_main = 2 * dc + 4 * dm
            w_main = w_in[:, :n_main].astype(BF16)
            wg = w_in[:, n_main:].reshape(d, 4, n_heads // 2, 2)
            w_gate_t = wg.transpose(2, 1, 3, 0).reshape(4 * n_heads, d).astype(BF16)
            gate_b_col = mlstm_gate_b[e].reshape(4, n_heads // 2, 2).transpose(1, 0, 2).reshape(4 * n_heads, 1)
            u, q, k, v, og, gt = _even_proj(xx, mod(l, 0), mod(l, 1), row2(mix_pre_g[l]), w_main, w_gate_t,
                                            gate_b_col, n_batch=bsz, dc=dc, dm=dm, hd=hd_m)
            mm = _mlstm(q, k, v, og, gt, row2(mlstm_norm_g[e]), hd=hd_m, n_ctx_chunks=ctx_len // MLSTM_CHUNK)
            xx = _even_out(u, mm, xx, mod(l, 2), row2(mix_post_g[l]), conv_dw_w[e], row2(conv_ln_g[e]),
                           row2(conv_ln_b[e]), even_w_out[e].astype(BF16), n_batch=bsz)
        else:
            o = l // 2
            w_in = odd_w_in[o]
            w_pool = w_in[:, :dp].astype(BF16)
            w_qk = w_in[:, dp:dp + (n_q + n_kv) * hd].reshape(d, n_q + n_kv, hd)[:, :, perm]
            w_qk_t = w_qk.reshape(d, -1).T.astype(BF16)
            w_v_t = w_in[:, dp + (n_q + n_kv) * hd:].T.astype(BF16)
            pool_u, q_t, k, v_t = _odd_proj(xx, mod(l, 0), mod(l, 1), row2(mix_pre_g[l]), w_pool, w_qk_t, w_v_t,
                                            cos_t, sin_t, q_norm_g[o][perm].reshape(hd, 1),
                                            k_norm_g[o][perm].reshape(hd, 1), n_batch=bsz, hd=hd, n_q=n_q, n_kv=n_kv)
            tile_off = 1 if last else 0
            attn = _attention(q_t, k, v_t, tile_off=tile_off)
            pg = dp // len(POOL_WINDOWS)
            pool_bd = jnp.zeros((dp, dp), F32)
            for gi in range(len(POOL_WINDOWS)):
                pool_bd = pool_bd.at[gi * pg:(gi + 1) * pg, gi * pg:(gi + 1) * pg].set(pool_w[o, gi])
            xx = _odd_out(pool_u, attn, xx, mod(l, 2), row2(mix_post_g[l]), pool_bd.astype(BF16),
                          row2(pool_scale[o]), odd_w_out[o].astype(BF16), n_batch=bsz, tile_off=tile_off,
                          seq_latent=seq)
        xx = _ffn(xx, mod(l, 3), mod(l, 4), mod(l, 5), row2(ffn_pre_g[l]), row2(ffn_post_g[l]),
                  ffn_w_in[l].astype(BF16), ffn_conv_w[l], ffn_w_out[l].astype(BF16), n_batch=bsz,
                  has_ctx=xx.shape[1] != seq)
    return xx if xx.shape[1] == seq else xx[:, ctx_len:]
```

```python
import functools

import jax
import jax.numpy as jnp
import numpy as np
from jax import lax
from jax.experimental import pallas as pl
from jax.experimental.pallas import tpu as pltpu

F32 = jnp.float32
BF16 = jnp.bfloat16

GRID_W = 64
MLSTM_CHUNK = 64
MLSTM_WIN = 2 * MLSTM_CHUNK
POOL_WINDOWS = (2, 4, 8, 16)
Q_PER_KV = 3
ROPE_THETA = 10000.0
EPS = 1e-6
LOG2_E = 1.4426950408889634

TM = 256
CONV_HALO = 16
SMALL_HALO = 8
FF_CHUNK = 256
ATTN_KEY_CHUNK = 256
VMEM_LIMIT = 56 * 1024 * 1024


def _cparams(n_axes, vmem=VMEM_LIMIT):
    return pltpu.CompilerParams(dimension_semantics=("arbitrary",) * n_axes, vmem_limit_bytes=vmem)


def _const_spec(shape):
    nd = len(shape)
    return pl.BlockSpec(shape, lambda *_: (0,) * nd)


def _rms(x, g):
    return x * lax.rsqrt(jnp.mean(x * x, axis=-1, keepdims=True) + EPS) * g


def _segment_halo_valid(j, n_tiles_total, has_ctx):
    if has_ctx:
        prev_ok = jnp.logical_and(j != 0, j != 1)
        next_ok = jnp.logical_and(j != 0, j != n_tiles_total - 1)
    else:
        prev_ok = j != 0
        next_ok = j != n_tiles_total - 1
    return prev_ok.astype(F32), next_ok.astype(F32)


def _ada_kernel(c_ref, w_ref, b_ref, o_ref):
    c = c_ref[...]
    s = c * jax.nn.sigmoid(c)
    o_ref[0] = jnp.dot(s, w_ref[0], preferred_element_type=F32, precision=lax.Precision.HIGHEST) + b_ref[0]


def _ada_all(cc, ada_w, ada_b):
    depth, d, n = ada_w.shape
    nb = n // 4
    rows = cc.shape[0]
    return pl.pallas_call(
        _ada_kernel,
        grid=(depth, n // nb),
        in_specs=[pl.BlockSpec((rows, d), lambda l, j: (0, 0)),
                  pl.BlockSpec((1, d, nb), lambda l, j: (l, 0, j)),
                  pl.BlockSpec((1, 1, nb), lambda l, j: (l, 0, j))],
        out_specs=pl.BlockSpec((1, rows, nb), lambda l, j: (l, 0, j)),
        out_shape=jax.ShapeDtypeStruct((depth, rows, n), F32),
        compiler_params=_cparams(2),
        name="ada_mod",
    )(cc, ada_w, ada_b.reshape(depth, 1, n))


def _mod_spec(d, off, n_batch):
    if off == 0:
        return pl.BlockSpec((1, 1, d), lambda b, i: (jnp.where(i == 0, n_batch, b), 0, 0))
    return pl.BlockSpec((1, 1, d), lambda b, i: (b, 0, 0))


def _even_proj_kernel(x_ref, sh_ref, sc_ref, g_ref, w_ref, wk_ref, wg_ref, gb_ref,
                      u_ref, q_ref, kt_ref, v_ref, o_ref, gt_ref, *, dc, dm, k_scale):
    x = x_ref[0]
    h = (_rms(x, g_ref[...]) * (1.0 + sc_ref[0]) + sh_ref[0]).astype(BF16)
    nt_dims = (((1,), (1,)), ((), ()))

    def proj(lo, hi):
        return jnp.dot(h, w_ref[:, lo:hi], preferred_element_type=F32)

    off = 2 * dc
    p_glu = proj(0, off)
    p_q = proj(off, off + dm)
    u_ref[0] = (p_glu[:, :dc] * jax.nn.sigmoid(p_glu[:, dc:])).astype(BF16)
    k_t = lax.dot_general(wk_ref[...], h, nt_dims, preferred_element_type=F32)
    q_ref[0] = p_q.astype(BF16)
    p_v = proj(off + 2 * dm, off + 3 * dm)
    kt_ref[0] = (k_t * k_scale).astype(BF16)
    p_o = proj(off + 3 * dm, off + 4 * dm)
    v_ref[0] = p_v.astype(BF16)
    gt = lax.dot_general(wg_ref[...], h, nt_dims, preferred_element_type=F32) + gb_ref[...]
    o_ref[0] = jax.nn.sigmoid(p_o).astype(BF16)
    row = lax.broadcasted_iota(jnp.int32, gt.shape, 0)
    gt = jnp.where((row & 2) != 0, jax.nn.log_sigmoid(gt), gt)
    for pr in range(gt.shape[0] // 8):
        for c in range(TM // MLSTM_WIN):
            gt_ref[0, pr, c] = gt[pr * 8:(pr + 1) * 8, c * MLSTM_WIN:(c + 1) * MLSTM_WIN]


def _even_proj(xx, shift, scale, pre_g, w_main, w_k_t, w_gate_t, gate_b_col, *, n_batch, dc, dm, hd):
    bsz, tt, d = xx.shape
    nt = tt // TM
    n_sub = TM // MLSTM_WIN
    n_pairs = w_gate_t.shape[0] // 8
    n_win = tt // MLSTM_WIN
    tok = lambda w: pl.BlockSpec((1, TM, w), lambda b, i: (b, i, 0))
    kern = functools.partial(_even_proj_kernel, dc=dc, dm=dm, k_scale=float(hd) ** -0.5)
    return pl.pallas_call(
        kern,
        grid=(bsz, nt),
        in_specs=[tok(d), _mod_spec(d, 0, n_batch), _mod_spec(d, 0, n_batch), _const_spec((1, d)),
                  _const_spec(w_main.shape), _const_spec(w_k_t.shape), _const_spec(w_gate_t.shape),
                  _const_spec(gate_b_col.shape)],
        out_specs=[tok(dc), tok(dm), pl.BlockSpec((1, dm, TM), lambda b, i: (b, 0, i)), tok(dm), tok(dm),
                   pl.BlockSpec((1, n_pairs, n_sub, 8, MLSTM_WIN), lambda b, i: (b, 0, i, 0, 0))],
        out_shape=[jax.ShapeDtypeStruct((bsz, tt, dc), BF16), jax.ShapeDtypeStruct((bsz, tt, dm), BF16),
                   jax.ShapeDtypeStruct((bsz, dm, tt), BF16), jax.ShapeDtypeStruct((bsz, tt, dm), BF16),
                   jax.ShapeDtypeStruct((bsz, tt, dm), BF16),
                   jax.ShapeDtypeStruct((bsz, n_pairs, n_win, 8, MLSTM_WIN), F32)],
        compiler_params=_cparams(2),
        name="even_proj",
    )(xx, shift, scale, pre_g, w_main, w_k_t, w_gate_t, gate_b_col)


def _mlstm_kernel(q_ref, kt_ref, v_ref, og_ref, gt_ref, ng_ref, out_ref,
                  hf_ref, hb_ref, c_ref, grow_ref, gmax_ref, blast_ref, mprev_ref,
                  *, hd, n_win, n_ctx_win, tt):
    L, W = MLSTM_CHUNK, MLSTM_WIN
    rowi = lax.broadcasted_iota(jnp.int32, (L, W), 0)
    lanei = lax.broadcasted_iota(jnp.int32, (L, W), 1)
    pos, lane_half = lanei & (L - 1), lanei >> 6
    tri = {(d, hf): jnp.logical_and(lane_half == hf, pos <= rowi if d == 0 else pos >= rowi)
           for d in range(2) for hf in range(2)}
    half1 = lax.broadcasted_iota(jnp.int32, (1, W), 1) >> 6
    ones_blk = jnp.ones((W, hd), BF16)
    half_order = ((0, 1), (1, 0))

    def win_of(d, t):
        if d == 0:
            return t
        return jnp.where(t < n_ctx_win, n_ctx_win - 1 - t, n_win - 1 - (t - n_ctx_win))

    sq_r = lax.broadcasted_iota(jnp.int32, (W, W), 0)
    sq_c = lax.broadcasted_iota(jnp.int32, (W, W), 1)
    same_chunk = (sq_r >> 6) == (sq_c >> 6)
    for d in range(2):
        order = (sq_r & (L - 1)) <= (sq_c & (L - 1)) if d == 0 else (sq_r & (L - 1)) >= (sq_c & (L - 1))
        cum = jnp.where(jnp.logical_and(same_chunk, order), 1.0, 0.0)
        for hh in range(2):
            ci = d * 2 + hh
            i_rows = gt_ref[0, 0, :, 4 * d + hh, :]
            lf_rows = gt_ref[0, 0, :, 4 * d + 2 + hh, :]
            b_rows = jnp.dot(lf_rows, cum, preferred_element_type=F32, precision=lax.Precision.HIGHEST)
            g_rows = i_rows - b_rows
            grow_ref[ci, :n_win, :] = g_rows
            for hf in range(2):
                own = half1 == hf
                gmax_ref[ci, hf, :n_win, :] = jnp.broadcast_to(
                    jnp.max(jnp.where(own, g_rows, -jnp.inf), axis=1, keepdims=True), (n_win, W))
                blast_ref[ci, hf, :n_win, :] = jnp.broadcast_to(
                    jnp.sum(jnp.where(own, lf_rows, 0.0), axis=1, keepdims=True), (n_win, W))

    def scan(t, ms):
        out = []
        for d in range(2):
            w = win_of(d, t)
            for hh in range(2):
                ci = d * 2 + hh
                m = ms[ci]
                for hf in half_order[d]:
                    mprev_ref[ci, hf, pl.ds(w, 1), :] = m
                    m = blast_ref[ci, hf, pl.ds(w, 1), :] + jnp.maximum(m, gmax_ref[ci, hf, pl.ds(w, 1), :])
                out.append(m)
        return tuple(out)

    lax.fori_loop(0, n_win, scan, (jnp.zeros((1, W), F32),) * 4)

    c_ref[...] = jnp.zeros(c_ref.shape, F32)

    def step(t, carry):
        chains = []
        for d in range(2):
            w = win_of(d, t)
            r0 = pl.multiple_of(w * W, W)
            gt = gt_ref[0, 0, w]
            for hh in range(2):
                ci = d * 2 + hh
                cols = slice(hh * hd, (hh + 1) * hd)
                lf_row = gt[4 * d + 2 + hh:4 * d + 3 + hh, :]
                g_row = grow_ref[ci, pl.ds(w, 1), :]
                kt_win = kt_ref[0, cols, pl.ds(r0, W)]
                ch = dict(ci=ci, r0=r0, cols=cols, h_ref=hf_ref if d == 0 else hb_ref, halves=half_order[d],
                          q_win=q_ref[0, pl.ds(r0, W), cols], kt_win=kt_win,
                          v_aug=jnp.concatenate([v_ref[0, pl.ds(r0, W), cols], ones_blk], axis=1))
                for hf in range(2):
                    m_prev = mprev_ref[ci, hf, pl.ds(w, 1), :][:, :1]
                    g_max = gmax_ref[ci, hf, pl.ds(w, 1), :][:, :1]
                    own = tri[(d, hf)]
                    b_col = jnp.sum(jnp.where(own, lf_row, 0.0), axis=1, keepdims=True)
                    cg_col = jnp.max(jnp.where(own, g_row, -jnp.inf), axis=1, keepdims=True)
                    m_col = jnp.broadcast_to(jnp.maximum(m_prev, cg_col), (L, W))
                    g_top = jnp.maximum(m_prev, g_max)
                    e_row = jnp.where(half1 == hf, jnp.exp(g_row - g_top), 0.0)
                    ch[hf] = dict(decay=jnp.where(own, jnp.exp(g_row - m_col), 0.0), w_col=jnp.exp(m_prev - m_col),
                                  clamp=jnp.exp(-(b_col + m_col)), keep=jnp.exp(m_prev - g_top),
                                  kte=(kt_win.astype(F32) * e_row).astype(BF16))
                chains.append(ch)
        for ch in chains:
            ch["qk"] = jnp.dot(ch["q_win"], ch["kt_win"], preferred_element_type=F32)
        for hf_i in range(2):
            for ch in chains:
                hf = ch["halves"][hf_i]
                ch[hf]["upd"] = jnp.dot(ch[hf]["kte"], ch["v_aug"], preferred_element_type=F32)
        for ch in chains:
            ch["c_aug"] = c_ref[ch["ci"]]
        for hf_i in range(2):
            for ch in chains:
                hf = ch["halves"][hf_i]
                rows = slice(hf * L, (hf + 1) * L)
                ch[hf]["qc"] = jnp.dot(ch["q_win"][rows], ch["c_aug"].astype(BF16), preferred_element_type=F32)
                ch[hf]["sv"] = jnp.dot((ch["qk"][rows] * ch[hf]["decay"]).astype(BF16), ch["v_aug"],
                                       preferred_element_type=F32)
                ch["c_aug"] = ch[hf]["keep"] * ch["c_aug"] + ch[hf]["upd"]
        for ch in chains:
            for hf in ch["halves"]:
                wgt, qc, sv = ch[hf]["w_col"], ch[hf]["qc"], ch[hf]["sv"]
                num = wgt * qc[:, :hd] + sv[:, :hd]
                den = wgt * qc[:, hd:] + sv[:, hd:]
                ch["h_ref"][pl.ds(ch["r0"] + hf * L, L), ch["cols"]] = (
                    num / jnp.maximum(jnp.abs(den), ch[hf]["clamp"]))
            c_ref[ch["ci"]] = ch["c_aug"]
        return carry

    assert hd == W
    lax.fori_loop(0, n_win, step, 0, unroll=2)

    def merge(t, carry):
        r0 = pl.multiple_of(t * TM, TM)
        hsum = hf_ref[pl.ds(r0, TM), :] + hb_ref[pl.ds(r0, TM), :]
        og = og_ref[0, pl.ds(r0, TM), :].astype(F32)
        ng = ng_ref[...]
        parts = []
        for hh in range(2):
            cols = slice(hh * hd, (hh + 1) * hd)
            parts.append(_rms(hsum[:, cols], ng[:, cols]))
        out_ref[0, pl.ds(r0, TM), :] = (jnp.concatenate(parts, axis=1) * og).astype(BF16)
        return carry

    lax.fori_loop(0, tt // TM, merge, 0)


def _mlstm(q, k_t, v, og, gt, norm_g, *, hd, n_ctx_win):
    bsz, tt, dm = q.shape
    n_pairs = gt.shape[1]
    n_win = gt.shape[2]
    pw = 2 * hd
    nw_pad = -(-n_win // 8) * 8
    tok = pl.BlockSpec((1, tt, pw), lambda b, p: (b, 0, p))
    kern = functools.partial(_mlstm_kernel, hd=hd, n_win=n_win, n_ctx_win=n_ctx_win, tt=tt)
    return pl.pallas_call(
        kern,
        grid=(bsz, n_pairs),
        in_specs=[tok, pl.BlockSpec((1, pw, tt), lambda b, p: (b, p, 0)), tok, tok,
                  pl.BlockSpec((1, 1, n_win, 8, MLSTM_WIN), lambda b, p: (b, p, 0, 0, 0)),
                  pl.BlockSpec((1, pw), lambda b, p: (0, p))],
        out_specs=tok,
        out_shape=jax.ShapeDtypeStruct((bsz, tt, dm), BF16),
        scratch_shapes=[pltpu.VMEM((tt, pw), F32), pltpu.VMEM((tt, pw), F32),
                        pltpu.VMEM((4, hd, 2 * hd), F32), pltpu.VMEM((4, nw_pad, MLSTM_WIN), F32),
                        pltpu.VMEM((4, 2, nw_pad, MLSTM_WIN), F32), pltpu.VMEM((4, 2, nw_pad, MLSTM_WIN), F32),
                        pltpu.VMEM((4, 2, nw_pad, MLSTM_WIN), F32)],
        compiler_params=_cparams(2),
        name="mlstm",
    )(q, k_t, v, og, gt, norm_g)


def _even_out_kernel(u_ref, up_ref, un_ref, dw_ref, lg_ref, lb_ref, mm_ref, w_ref, x_ref, gate_ref, pg_ref,
                     o_ref, sh_ref, *, n_tiles_total, width):
    i = pl.program_id(1)
    pv, nv = _segment_halo_valid(i, n_tiles_total, True)
    dcv = u_ref.shape[2]
    z_mix = jnp.dot(mm_ref[0], w_ref[dcv:, :], preferred_element_type=F32)
    uext = jnp.concatenate([up_ref[0].astype(F32) * pv, u_ref[0].astype(F32), un_ref[0].astype(F32) * nv], axis=0)
    dw = dw_ref[...]
    base = CONV_HALO - width // 2
    span = sh_ref.shape[1]
    for r in range(8):
        sh_ref[r] = uext[base + r:base + r + span]
    acc = jnp.zeros((TM, uext.shape[1]), F32)
    for j in range(width):
        acc = acc + dw[j:j + 1, :] * sh_ref[j % 8, 8 * (j // 8):8 * (j // 8) + TM, :]
    mu = jnp.mean(acc, axis=-1, keepdims=True)
    cen = acc - mu
    var = jnp.mean(cen * cen, axis=-1, keepdims=True)
    y = cen * lax.rsqrt(var + EPS) * lg_ref[...] + lb_ref[...]
    conv_out = (y * jax.nn.sigmoid(y)).astype(BF16)
    z = z_mix + jnp.dot(conv_out, w_ref[:dcv, :], preferred_element_type=F32)
    o_ref[0] = x_ref[0] + gate_ref[0] * _rms(z, pg_ref[...])


def _even_out(u, mm, xx, gate, post_g, dw_w, ln_g, ln_b, w_out, *, n_batch):
    bsz, tt, d = xx.shape
    nt = tt // TM
    dc = u.shape[-1]
    hb = TM // CONV_HALO
    n_hblk = tt // CONV_HALO
    tok = lambda w: pl.BlockSpec((1, TM, w), lambda b, i: (b, i, 0))
    kern = functools.partial(_even_out_kernel, n_tiles_total=nt, width=dw_w.shape[0])
    return pl.pallas_call(
        kern,
        grid=(bsz, nt),
        in_specs=[tok(dc),
                  pl.BlockSpec((1, CONV_HALO, dc), lambda b, i: (b, jnp.maximum(i * hb - 1, 0), 0)),
                  pl.BlockSpec((1, CONV_HALO, dc), lambda b, i: (b, jnp.minimum((i + 1) * hb, n_hblk - 1), 0)),
                  _const_spec(dw_w.shape), _const_spec((1, dc)), _const_spec((1, dc)),
                  tok(mm.shape[-1]), _const_spec(w_out.shape), tok(d), _mod_spec(d, 0, n_batch),
                  _const_spec((1, d))],
        out_specs=tok(d),
        out_shape=jax.ShapeDtypeStruct((bsz, tt, d), F32),
        scratch_shapes=[pltpu.VMEM((8, TM + 8 * ((dw_w.shape[0] - 1) // 8), dc), F32)],
        compiler_params=_cparams(2),
        name="even_out",
    )(u, u, u, dw_w, ln_g, ln_b, mm, w_out, xx, gate, post_g)


def _odd_proj_kernel(x_ref, sh_ref, sc_ref, g_ref, wp_ref, wqk_ref, wv_ref, cos_ref, sin_ref, qg_ref, kg_ref,
                     pool_ref, qt_ref, k_ref, vt_ref, *, hd, n_q, n_kv, q_scale):
    x = x_ref[0]
    h = (_rms(x, g_ref[...]) * (1.0 + sc_ref[0]) + sh_ref[0]).astype(BF16)
    nt_dims = (((1,), (1,)), ((), ()))
    n_heads = n_q + n_kv
    split = n_heads // 2
    pool = jnp.dot(h, wp_ref[...], preferred_element_type=F32)
    qk_parts = [lax.dot_general(wqk_ref[lo * hd:hi * hd, :], h, nt_dims, preferred_element_type=F32)
                for lo, hi in ((0, split), (split, n_heads))]
    v_t = lax.dot_general(wv_ref[...], h, nt_dims, preferred_element_type=F32)
    pool_ref[0] = pool
    cos = cos_ref[...]
    sin = sin_ref[...]
    half = hd // 2
    for hi in range(n_heads):
        part, base = (qk_parts[0], 0) if hi < split else (qk_parts[1], split)
        t = part[(hi - base) * hd:(hi - base + 1) * hd]
        gain = qg_ref[...] if hi < n_q else kg_ref[...]
        tn = t * lax.rsqrt(jnp.mean(t * t, axis=0, keepdims=True) + EPS) * gain
        x1 = tn[:half]
        x2 = tn[half:]
        rot = jnp.concatenate([x1 * cos - x2 * sin, x1 * sin + x2 * cos], axis=0)
        if hi < n_q:
            kvh, gq = divmod(hi, Q_PER_KV)
            qt_ref[0, kvh, 0, :, gq * TM:(gq + 1) * TM] = (rot * q_scale).astype(BF16)
        else:
            k_ref[0, hi - n_q] = rot.T.astype(BF16)
    for kvh in range(n_kv):
        vt_ref[0, kvh] = v_t[kvh * hd:(kvh + 1) * hd].astype(BF16)


def _odd_proj(xx, shift, scale, pre_g, w_pool, w_qk_t, w_v_t, cos_t, sin_t, qg_col, kg_col, *, n_batch, hd, n_q, n_kv):
    bsz, tt, d = xx.shape
    nt = tt // TM
    dp = w_pool.shape[1]
    kern = functools.partial(_odd_proj_kernel, hd=hd, n_q=n_q, n_kv=n_kv, q_scale=float(hd) ** -0.5 * LOG2_E)
    return pl.pallas_call(
        kern,
        grid=(bsz, nt),
        in_specs=[pl.BlockSpec((1, TM, d), lambda b, i: (b, i, 0)),
                  _mod_spec(d, 0, n_batch), _mod_spec(d, 0, n_batch), _const_spec((1, d)),
                  _const_spec(w_pool.shape), _const_spec(w_qk_t.shape), _const_spec(w_v_t.shape),
                  pl.BlockSpec((hd // 2, TM), lambda b, i: (0, i)), pl.BlockSpec((hd // 2, TM), lambda b, i: (0, i)),
                  _const_spec((hd, 1)), _const_spec((hd, 1))],
        out_specs=[pl.BlockSpec((1, TM, dp), lambda b, i: (b, i, 0)),
                   pl.BlockSpec((1, n_kv, 1, hd, Q_PER_KV * TM), lambda b, i: (b, 0, i, 0, 0)),
                   pl.BlockSpec((1, n_kv, TM, hd), lambda b, i: (b, 0, i, 0)),
                   pl.BlockSpec((1, n_kv, hd, TM), lambda b, i: (b, 0, 0, i))],
        out_shape=[jax.ShapeDtypeStruct((bsz, tt, dp), F32),
                   jax.ShapeDtypeStruct((bsz, n_kv, nt, hd, Q_PER_KV * TM), BF16),
                   jax.ShapeDtypeStruct((bsz, n_kv, tt, hd), BF16),
                   jax.ShapeDtypeStruct((bsz, n_kv, hd, tt), BF16)],
        compiler_params=_cparams(2),
        name="odd_proj",
    )(xx, shift, scale, pre_g, w_pool, w_qk_t, w_v_t, cos_t, sin_t, qg_col, kg_col)


def _attn_kernel(qt_ref, k_ref, vt_ref, o_ref, *, hd, tile_off):
    n_kv = k_ref.shape[1]
    heads = [(kvh, g) for kvh in range(n_kv) for g in range(Q_PER_KV)]

    def attend(n_keys):
        kc = ATTN_KEY_CHUNK
        n_kc = n_keys // kc

        def fold8(a, op):
            return op(a.reshape(kc // 8, 8, TM), axis=0)

        def scores(kvh, g):
            q_t = qt_ref[0, kvh, 0, :, g * TM:(g + 1) * TM]
            chunks, mx = [], None
            for j in range(n_kc):
                s = jnp.dot(k_ref[0, kvh, j * kc:(j + 1) * kc, :], q_t, preferred_element_type=F32)
                chunks.append(s)
                m8 = fold8(s, jnp.max)
                mx = m8 if mx is None else jnp.maximum(mx, m8)
            return chunks, jnp.max(mx, axis=0, keepdims=True)

        nxt = scores(*heads[0])
        for n, (kvh, g) in enumerate(heads):
            chunks, mx = nxt
            if n + 1 < len(heads):
                nxt = scores(*heads[n + 1])
            den8, probs = None, []
            for s in chunks:
                p = jnp.exp2(s - mx)
                d8 = fold8(p, jnp.sum)
                den8 = d8 if den8 is None else den8 + d8
                probs.append(p.astype(BF16))
            den = jnp.sum(den8, axis=0, keepdims=True)
            o_t = jnp.dot(vt_ref[0, kvh, :, :n_keys], jnp.concatenate(probs, axis=0),
                          preferred_element_type=F32)
            col = (kvh * Q_PER_KV + g) * hd
            o_ref[0, :, col:col + hd] = (o_t / den).T.astype(BF16)

    if tile_off == 0:
        i = pl.program_id(1)

        @pl.when(i == 0)
        def _():
            attend(TM)

        @pl.when(i != 0)
        def _():
            attend(k_ref.shape[2])
    else:
        attend(k_ref.shape[2])


def _attention(q_t, k, v_t, *, tile_off):
    bsz, n_kv, nt, hd, qw = q_t.shape
    tt = k.shape[2]
    nq = nt - tile_off
    kern = functools.partial(_attn_kernel, hd=hd, tile_off=tile_off)
    return pl.pallas_call(
        kern,
        grid=(bsz, nq),
        in_specs=[pl.BlockSpec((1, n_kv, 1, hd, qw), lambda b, i: (b, 0, i + tile_off, 0, 0)),
                  pl.BlockSpec((1, n_kv, tt, hd), lambda b, i: (b, 0, 0, 0)),
                  pl.BlockSpec((1, n_kv, hd, tt), lambda b, i: (b, 0, 0, 0))],
        out_specs=pl.BlockSpec((1, TM, n_kv * Q_PER_KV * hd), lambda b, i: (b, i, 0)),
        out_shape=jax.ShapeDtypeStruct((bsz, nq * TM, n_kv * Q_PER_KV * hd), BF16),
        compiler_params=_cparams(2),
        name="gqa_attention",
    )(q_t, k, v_t)


def _odd_out_kernel(u_ref, up_ref, un_ref, pw_ref, ps_ref, at_ref, w_ref, x_ref, gate_ref, pg_ref, o_ref,
                    *, n_tiles_total, tile_off, pool_group, seq_latent):
    i = pl.program_id(1)
    j = i + tile_off
    pv, nv = _segment_halo_valid(j, n_tiles_total, True)
    u = u_ref[0]
    dpl = u.shape[1]
    z_attn = jnp.dot(at_ref[0], w_ref[dpl:, :], preferred_element_type=F32)
    uext = jnp.concatenate([up_ref[0] * pv, u, un_ref[0] * nv], axis=0)
    n_ext = uext.shape[0]
    s2 = uext[:n_ext - 1] + uext[1:]
    s4 = s2[:n_ext - 3] + s2[2:]
    s8 = s4[:n_ext - 7] + s4[4:]
    s16 = s8[:n_ext - 15] + s8[8:]
    sums = (s2, s4, s8, s16)
    seg_start = jnp.where(j == 0, 0, 1)
    seg_len = jnp.where(j == 0, TM, seq_latent).astype(F32)
    t = ((j - seg_start) * TM + lax.broadcasted_iota(jnp.int32, (TM, 1), 0)).astype(F32)
    lane_group = lax.broadcasted_iota(jnp.int32, (1, u.shape[1]), 1) // pool_group
    mean = jnp.zeros_like(u)
    for gi, w in enumerate(POOL_WINDOWS):
        hw = w // 2
        win = sums[gi][SMALL_HALO - hw:SMALL_HALO - hw + TM]
        cnt = jnp.minimum(t + hw, seg_len) - jnp.maximum(t - hw, 0.0)
        mean = jnp.where(lane_group == gi, win / cnt, mean)
    dpool = (mean - u).astype(BF16)
    y = jnp.dot(dpool, pw_ref[...], preferred_element_type=F32) * ps_ref[...]
    z = z_attn + jnp.dot(y.astype(BF16), w_ref[:dpl, :], preferred_element_type=F32)
    o_ref[0] = x_ref[0] + gate_ref[0] * _rms(z, pg_ref[...])


def _odd_out(pool_u, attn, xx, gate, post_g, pool_bd, pool_scale, w_out, *, n_batch, tile_off, seq_latent):
    bsz, tt, d = xx.shape
    nt = tt // TM
    n_out = nt - tile_off
    dp = pool_u.shape[-1]
    hb = TM // SMALL_HALO
    n_hblk = tt // SMALL_HALO
    kern = functools.partial(_odd_out_kernel, n_tiles_total=nt, tile_off=tile_off,
                             pool_group=dp // len(POOL_WINDOWS), seq_latent=seq_latent)
    return pl.pallas_call(
        kern,
        grid=(bsz, n_out),
        in_specs=[pl.BlockSpec((1, TM, dp), lambda b, i: (b, i + tile_off, 0)),
                  pl.BlockSpec((1, SMALL_HALO, dp), lambda b, i: (b, jnp.maximum((i + tile_off) * hb - 1, 0), 0)),
                  pl.BlockSpec((1, SMALL_HALO, dp),
                               lambda b, i: (b, jnp.minimum((i + tile_off + 1) * hb, n_hblk - 1), 0)),
                  _const_spec(pool_bd.shape), _const_spec((1, dp)),
                  pl.BlockSpec((1, TM, attn.shape[-1]), lambda b, i: (b, i, 0)),
                  _const_spec(w_out.shape),
                  pl.BlockSpec((1, TM, d), lambda b, i: (b, i + tile_off, 0)),
                  _mod_spec(d, tile_off, n_batch), _const_spec((1, d))],
        out_specs=pl.BlockSpec((1, TM, d), lambda b, i: (b, i, 0)),
        out_shape=jax.ShapeDtypeStruct((bsz, n_out * TM, d), F32),
        compiler_params=_cparams(2),
        name="odd_out",
    )(pool_u, pool_u, pool_u, pool_bd, pool_scale, attn, w_out, xx, gate, post_g)


def _ffn_kernel(x_ref, xp_ref, xn_ref, sh_ref, sc_ref, gate_ref, g_ref, pg_ref, wi_ref, cw_ref, wo_ref, o_ref,
                *, n_tiles_total, has_ctx):
    d_ff = wo_ref.shape[0]
    n_chunks = d_ff // FF_CHUNK
    i = pl.program_id(1)
    pv, nv = _segment_halo_valid(i, n_tiles_total, has_ctx)
    xc = x_ref[0]
    xe = jnp.concatenate([xp_ref[0], xc, xn_ref[0]], axis=0)
    he = _rms(xe, g_ref[...]) * (1.0 + sc_ref[0]) + sh_ref[0]
    row = lax.broadcasted_iota(jnp.int32, (xe.shape[0], 1), 0)
    keep = jnp.where(row < SMALL_HALO, pv, jnp.where(row >= SMALL_HALO + TM, nv, 1.0))
    he = (he * keep).astype(BF16)
    hc = he[SMALL_HALO:SMALL_HALO + TM]

    def up_proj(c):
        lo = c * FF_CHUNK
        return (jnp.dot(he, wi_ref[:, lo:lo + FF_CHUNK], preferred_element_type=F32),
                jnp.dot(hc, wi_ref[:, d_ff + lo:d_ff + lo + FF_CHUNK], preferred_element_type=F32))

    acc = jnp.zeros((TM, o_ref.shape[2]), F32)
    nxt = up_proj(0)
    for c in range(n_chunks):
        g, v = nxt
        if c + 1 < n_chunks:
            nxt = up_proj(c + 1)
        cw = cw_ref[:, c * FF_CHUNK:(c + 1) * FF_CHUNK]
        gc = (cw[0:1] * g[SMALL_HALO - 1:SMALL_HALO - 1 + TM] + cw[1:2] * g[SMALL_HALO:SMALL_HALO + TM]
              + cw[2:3] * g[SMALL_HALO + 1:SMALL_HALO + 1 + TM])
        u = (gc * jax.nn.sigmoid(gc) * v).astype(BF16)
        acc = acc + jnp.dot(u, wo_ref[c * FF_CHUNK:(c + 1) * FF_CHUNK, :], preferred_element_type=F32)
    o_ref[0] = xc + gate_ref[0] * _rms(acc, pg_ref[...])


def _ffn(xx, shift, scale, gate, pre_g, post_g, w_in, conv_w, w_out, *, n_batch, has_ctx):
    bsz, tt, d = xx.shape
    nt = tt // TM
    hb = TM // SMALL_HALO
    n_hblk = tt // SMALL_HALO
    off = 0 if has_ctx else 1
    kern = functools.partial(_ffn_kernel, n_tiles_total=nt, has_ctx=has_ctx)
    return pl.pallas_call(
        kern,
        grid=(bsz, nt),
        in_specs=[pl.BlockSpec((1, TM, d), lambda b, i: (b, i, 0)),
                  pl.BlockSpec((1, SMALL_HALO, d), lambda b, i: (b, jnp.maximum(i * hb - 1, 0), 0)),
                  pl.BlockSpec((1, SMALL_HALO, d), lambda b, i: (b, jnp.minimum((i + 1) * hb, n_hblk - 1), 0)),
                  _mod_spec(d, off, n_batch), _mod_spec(d, off, n_batch), _mod_spec(d, off, n_batch),
                  _const_spec((1, d)), _const_spec((1, d)),
                  _const_spec(w_in.shape), _const_spec(conv_w.shape), _const_spec(w_out.shape)],
        out_specs=pl.BlockSpec((1, TM, d), lambda b, i: (b, i, 0)),
        out_shape=jax.ShapeDtypeStruct((bsz, tt, d), F32),
        compiler_params=_cparams(2),
        name="conv_ffn",
    )(xx, xx, xx, shift, scale, gate, pre_g, post_g, w_in, conv_w, w_out)


def _rope_tables_t(seq, ctx_len, hd):
    rows = seq // GRID_W
    row = jnp.repeat(jnp.arange(rows), GRID_W).astype(F32)
    col = jnp.tile(jnp.arange(GRID_W), rows).astype(F32)
    n_freq = hd // 4
    inv = ROPE_THETA ** (-jnp.arange(n_freq, dtype=F32) / n_freq)
    ang = jnp.concatenate([row[:, None] * inv, col[:, None] * inv], axis=-1)
    cos = jnp.concatenate([jnp.ones((ctx_len, hd // 2), F32), jnp.cos(ang)], axis=0)
    sin = jnp.concatenate([jnp.zeros((ctx_len, hd // 2), F32), jnp.sin(ang)], axis=0)
    return cos.T, sin.T


def kernel(x, c, ctx, c_ctx, ada_w, ada_b, mix_pre_g, mix_post_g, ffn_pre_g, ffn_post_g, ffn_w_in, ffn_conv_w,
           ffn_w_out, even_w_in, even_w_out, conv_dw_w, conv_ln_g, conv_ln_b, mlstm_gate_b, mlstm_norm_g,
           odd_w_in, odd_w_out, pool_w, pool_scale, q_norm_g, k_norm_g):
    bsz, seq, d = x.shape
    ctx_len = ctx.shape[1]
    depth = ada_w.shape[0]
    assert ctx_len == TM and seq % TM == 0 and seq % GRID_W == 0
    dc = conv_dw_w.shape[2]
    dm = mlstm_norm_g.shape[1]
    n_heads = mlstm_gate_b.shape[2]
    hd_m = dm // n_heads
    assert n_heads % 2 == 0 and even_w_in.shape[2] == 2 * dc + 4 * dm + 4 * n_heads
    hd = q_norm_g.shape[1]
    dp = pool_scale.shape[1]
    n_kv = (odd_w_in.shape[2] - dp - (odd_w_out.shape[1] - dp)) // (2 * hd)
    n_q = (odd_w_out.shape[1] - dp) // hd
    assert n_q == n_kv * Q_PER_KV
    assert ffn_w_out.shape[1] % FF_CHUNK == 0

    pad = (-(bsz + 1)) % 8
    cc = jnp.concatenate([c, c_ctx[None, :], jnp.zeros((pad, d), F32)], axis=0)
    mods = _ada_all(cc, ada_w, ada_b)

    def mod(l, k):
        return mods[l, :bsz + 1, k * d:(k + 1) * d].reshape(bsz + 1, 1, d)

    cos_t, sin_t = _rope_tables_t(seq, ctx_len, hd)
    perm = np.concatenate([np.arange(0, hd, 2), np.arange(1, hd, 2)])

    xx = jnp.concatenate([ctx, x], axis=1)
    row2 = lambda a: a.reshape(1, -1)

    for l in range(depth):
        last = l == depth - 1
        if l % 2 == 0:
            e = l // 2
            w_in = even_w_in[e]
            n_main = 2 * dc + 4 * dm
            w_main = w_in[:, :n_main].astype(BF16)
            wg = w_in[:, n_main:].reshape(d, 4, n_heads // 2, 2)
            w_gate_t = wg.transpose(2, 1, 3, 0).reshape(4 * n_heads, d).astype(BF16)
            gate_b_col = mlstm_gate_b[e].reshape(4, n_heads // 2, 2).transpose(1, 0, 2).reshape(4 * n_heads, 1)
            w_k_t = w_in[:, 2 * dc + dm:2 * dc + 2 * dm].T.astype(BF16)
            u, q, k_t, v, og, gt = _even_proj(xx, mod(l, 0), mod(l, 1), row2(mix_pre_g[l]), w_main, w_k_t, w_gate_t,
                                              gate_b_col, n_batch=bsz, dc=dc, dm=dm, hd=hd_m)
            mm = _mlstm(q, k_t, v, og, gt, row2(mlstm_norm_g[e]), hd=hd_m, n_ctx_win=ctx_len // MLSTM_WIN)
            xx = _even_out(u, mm, xx, mod(l, 2), row2(mix_post_g[l]), conv_dw_w[e], row2(conv_ln_g[e]),
                           row2(conv_ln_b[e]), even_w_out[e].astype(BF16), n_batch=bsz)
        else:
            o = l // 2
            w_in = odd_w_in[o]
            w_pool = w_in[:, :dp].astype(BF16)
            w_qk = w_in[:, dp:dp + (n_q + n_kv) * hd].reshape(d, n_q + n_kv, hd)[:, :, perm]
            w_qk_t = w_qk.reshape(d, -1).T.astype(BF16)
            w_v_t = w_in[:, dp + (n_q + n_kv) * hd:].T.astype(BF16)
            pool_u, q_t, k, v_t = _odd_proj(xx, mod(l, 0), mod(l, 1), row2(mix_pre_g[l]), w_pool, w_qk_t, w_v_t,
                                            cos_t, sin_t, q_norm_g[o][perm].reshape(hd, 1),
                                            k_norm_g[o][perm].reshape(hd, 1), n_batch=bsz, hd=hd, n_q=n_q, n_kv=n_kv)
            tile_off = 1 if last else 0
            attn = _attention(q_t, k, v_t, tile_off=tile_off)
            pg = dp // len(POOL_WINDOWS)
            pool_bd = jnp.zeros((dp, dp), F32)
            for gi in range(len(POOL_WINDOWS)):
                pool_bd = pool_bd.at[gi * pg:(gi + 1) * pg, gi * pg:(gi + 1) * pg].set(pool_w[o, gi])
            xx = _odd_out(pool_u, attn, xx, mod(l, 2), row2(mix_post_g[l]), pool_bd.astype(BF16),
                          row2(pool_scale[o]), odd_w_out[o].astype(BF16), n_batch=bsz, tile_off=tile_off,
                          seq_latent=seq)
        xx = _ffn(xx, mod(l, 3), mod(l, 4), mod(l, 5), row2(ffn_pre_g[l]), row2(ffn_post_g[l]),
                  ffn_w_in[l].astype(BF16), ffn_conv_w[l], ffn_w_out[l].astype(BF16), n_batch=bsz,
                  has_ctx=xx.shape[1] != seq)
    return xx if xx.shape[1] == seq else xx[:, ctx_len:]
```

```python
import functools

import jax
import jax.numpy as jnp
import numpy as np
from jax import lax
from jax.experimental import pallas as pl
from jax.experimental.pallas import tpu as pltpu

F32 = jnp.float32
BF16 = jnp.bfloat16

GRID_W = 64
MLSTM_CHUNK = 64
MLSTM_WIN = 2 * MLSTM_CHUNK
POOL_WINDOWS = (2, 4, 8, 16)
Q_PER_KV = 3
ROPE_THETA = 10000.0
EPS = 1e-6
LOG2_E = 1.4426950408889634

TM = 256
CONV_HALO = 16
SMALL_HALO = 8
FF_CHUNK = 256
ATTN_KEY_CHUNK = 256
VMEM_LIMIT = 56 * 1024 * 1024


def _cparams(n_axes, vmem=VMEM_LIMIT):
    return pltpu.CompilerParams(dimension_semantics=("arbitrary",) * n_axes, vmem_limit_bytes=vmem)


def _const_spec(shape):
    nd = len(shape)
    return pl.BlockSpec(shape, lambda *_: (0,) * nd)


def _rms(x, g):
    return x * lax.rsqrt(jnp.mean(x * x, axis=-1, keepdims=True) + EPS) * g


def _segment_halo_valid(j, n_tiles_total, has_ctx):
    if has_ctx:
        prev_ok = jnp.logical_and(j != 0, j != 1)
        next_ok = jnp.logical_and(j != 0, j != n_tiles_total - 1)
    else:
        prev_ok = j != 0
        next_ok = j != n_tiles_total - 1
    return prev_ok.astype(F32), next_ok.astype(F32)


def _ada_kernel(c_ref, w_ref, b_ref, o_ref):
    c = c_ref[...]
    s = c * jax.nn.sigmoid(c)
    o_ref[0] = jnp.dot(s, w_ref[0], preferred_element_type=F32, precision=lax.Precision.HIGHEST) + b_ref[0]


def _ada_all(cc, ada_w, ada_b):
    depth, d, n = ada_w.shape
    nb = n // 4
    rows = cc.shape[0]
    return pl.pallas_call(
        _ada_kernel,
        grid=(depth, n // nb),
        in_specs=[pl.BlockSpec((rows, d), lambda l, j: (0, 0)),
                  pl.BlockSpec((1, d, nb), lambda l, j: (l, 0, j)),
                  pl.BlockSpec((1, 1, nb), lambda l, j: (l, 0, j))],
        out_specs=pl.BlockSpec((1, rows, nb), lambda l, j: (l, 0, j)),
        out_shape=jax.ShapeDtypeStruct((depth, rows, n), F32),
        compiler_params=_cparams(2),
        name="ada_mod",
    )(cc, ada_w, ada_b.reshape(depth, 1, n))


def _mod_spec(d, off, n_batch):
    if off == 0:
        return pl.BlockSpec((1, 1, d), lambda b, i: (jnp.where(i == 0, n_batch, b), 0, 0))
    return pl.BlockSpec((1, 1, d), lambda b, i: (b, 0, 0))


def _even_proj_kernel(x_ref, sh_ref, sc_ref, g_ref, w_ref, wk_ref, wg_ref, gb_ref,
                      u_ref, q_ref, kt_ref, v_ref, o_ref, gt_ref, *, dc, dm, k_scale):
    x = x_ref[0]
    h = (_rms(x, g_ref[...]) * (1.0 + sc_ref[0]) + sh_ref[0]).astype(BF16)
    nt_dims = (((1,), (1,)), ((), ()))

    def proj(lo, hi):
        return jnp.dot(h, w_ref[:, lo:hi], preferred_element_type=F32)

    off = 2 * dc
    p_glu = proj(0, off)
    p_q = proj(off, off + dm)
    u_ref[0] = (p_glu[:, :dc] * jax.nn.sigmoid(p_glu[:, dc:])).astype(BF16)
    k_t = lax.dot_general(wk_ref[...], h, nt_dims, preferred_element_type=F32)
    q_ref[0] = p_q.astype(BF16)
    p_v = proj(off + 2 * dm, off + 3 * dm)
    kt_ref[0] = (k_t * k_scale).astype(BF16)
    p_o = proj(off + 3 * dm, off + 4 * dm)
    v_ref[0] = p_v.astype(BF16)
    gt = lax.dot_general(wg_ref[...], h, nt_dims, preferred_element_type=F32) + gb_ref[...]
    o_ref[0] = jax.nn.sigmoid(p_o).astype(BF16)
    row = lax.broadcasted_iota(jnp.int32, gt.shape, 0)
    gt = jnp.where((row & 2) != 0, jax.nn.log_sigmoid(gt), gt)
    for pr in range(gt.shape[0] // 8):
        for c in range(TM // MLSTM_WIN):
            gt_ref[0, pr, c] = gt[pr * 8:(pr + 1) * 8, c * MLSTM_WIN:(c + 1) * MLSTM_WIN]


def _even_proj(xx, shift, scale, pre_g, w_main, w_k_t, w_gate_t, gate_b_col, *, n_batch, dc, dm, hd):
    bsz, tt, d = xx.shape
    nt = tt // TM
    n_sub = TM // MLSTM_WIN
    n_pairs = w_gate_t.shape[0] // 8
    n_win = tt // MLSTM_WIN
    tok = lambda w: pl.BlockSpec((1, TM, w), lambda b, i: (b, i, 0))
    kern = functools.partial(_even_proj_kernel, dc=dc, dm=dm, k_scale=float(hd) ** -0.5)
    return pl.pallas_call(
        kern,
        grid=(bsz, nt),
        in_specs=[tok(d), _mod_spec(d, 0, n_batch), _mod_spec(d, 0, n_batch), _const_spec((1, d)),
                  _const_spec(w_main.shape), _const_spec(w_k_t.shape), _const_spec(w_gate_t.shape),
                  _const_spec(gate_b_col.shape)],
        out_specs=[tok(dc), tok(dm), pl.BlockSpec((1, dm, TM), lambda b, i: (b, 0, i)), tok(dm), tok(dm),
                   pl.BlockSpec((1, n_pairs, n_sub, 8, MLSTM_WIN), lambda b, i: (b, 0, i, 0, 0))],
        out_shape=[jax.ShapeDtypeStruct((bsz, tt, dc), BF16), jax.ShapeDtypeStruct((bsz, tt, dm), BF16),
                   jax.ShapeDtypeStruct((bsz, dm, tt), BF16), jax.ShapeDtypeStruct((bsz, tt, dm), BF16),
                   jax.ShapeDtypeStruct((bsz, tt, dm), BF16),
                   jax.ShapeDtypeStruct((bsz, n_pairs, n_win, 8, MLSTM_WIN), F32)],
        compiler_params=_cparams(2),
        name="even_proj",
    )(xx, shift, scale, pre_g, w_main, w_k_t, w_gate_t, gate_b_col)


def _mlstm_kernel(q_ref, kt_ref, v_ref, og_ref, gt_ref, ng_ref, out_ref,
                  hf_ref, hb_ref, c_ref, grow_ref, gmax_ref, blast_ref, mprev_ref,
                  *, hd, n_win, n_ctx_win, tt):
    L, W = MLSTM_CHUNK, MLSTM_WIN
    rowi = lax.broadcasted_iota(jnp.int32, (L, W), 0)
    lanei = lax.broadcasted_iota(jnp.int32, (L, W), 1)
    pos, lane_half = lanei & (L - 1), lanei >> 6
    tri = {(d, hf): jnp.logical_and(lane_half == hf, pos <= rowi if d == 0 else pos >= rowi)
           for d in range(2) for hf in range(2)}
    half1 = lax.broadcasted_iota(jnp.int32, (1, W), 1) >> 6
    ones_blk = jnp.ones((W, hd), BF16)
    half_order = ((0, 1), (1, 0))

    def win_of(d, t):
        if d == 0:
            return t
        return jnp.where(t < n_ctx_win, n_ctx_win - 1 - t, n_win - 1 - (t - n_ctx_win))

    sq_r = lax.broadcasted_iota(jnp.int32, (W, W), 0)
    sq_c = lax.broadcasted_iota(jnp.int32, (W, W), 1)
    same_chunk = (sq_r >> 6) == (sq_c >> 6)
    for d in range(2):
        order = (sq_r & (L - 1)) <= (sq_c & (L - 1)) if d == 0 else (sq_r & (L - 1)) >= (sq_c & (L - 1))
        cum = jnp.where(jnp.logical_and(same_chunk, order), 1.0, 0.0)
        for hh in range(2):
            ci = d * 2 + hh
            i_rows = gt_ref[0, 0, :, 4 * d + hh, :]
            lf_rows = gt_ref[0, 0, :, 4 * d + 2 + hh, :]
            b_rows = jnp.dot(lf_rows, cum, preferred_element_type=F32, precision=lax.Precision.HIGHEST)
            g_rows = i_rows - b_rows
            grow_ref[ci, :n_win, :] = g_rows
            for hf in range(2):
                own = half1 == hf
                gmax_ref[ci, hf, :n_win, :] = jnp.broadcast_to(
                    jnp.max(jnp.where(own, g_rows, -jnp.inf), axis=1, keepdims=True), (n_win, W))
                blast_ref[ci, hf, :n_win, :] = jnp.broadcast_to(
                    jnp.sum(jnp.where(own, lf_rows, 0.0), axis=1, keepdims=True), (n_win, W))

    def scan(t, ms):
        out = []
        for d in range(2):
            w = win_of(d, t)
            for hh in range(2):
                ci = d * 2 + hh
                m = ms[ci]
                for hf in half_order[d]:
                    mprev_ref[ci, hf, pl.ds(w, 1), :] = m
                    m = blast_ref[ci, hf, pl.ds(w, 1), :] + jnp.maximum(m, gmax_ref[ci, hf, pl.ds(w, 1), :])
                out.append(m)
        return tuple(out)

    lax.fori_loop(0, n_win, scan, (jnp.zeros((1, W), F32),) * 4)

    c_ref[...] = jnp.zeros(c_ref.shape, F32)

    def step(t, carry):
        chains = []
        for d in range(2):
            w = win_of(d, t)
            r0 = pl.multiple_of(w * W, W)
            gt = gt_ref[0, 0, w]
            for hh in range(2):
                ci = d * 2 + hh
                cols = slice(hh * hd, (hh + 1) * hd)
                lf_row = gt[4 * d + 2 + hh:4 * d + 3 + hh, :]
                g_row = grow_ref[ci, pl.ds(w, 1), :]
                kt_win = kt_ref[0, cols, pl.ds(r0, W)]
                ch = dict(ci=ci, r0=r0, cols=cols, h_ref=hf_ref if d == 0 else hb_ref, halves=half_order[d],
                          q_win=q_ref[0, pl.ds(r0, W), cols], kt_win=kt_win,
                          v_aug=jnp.concatenate([v_ref[0, pl.ds(r0, W), cols], ones_blk], axis=1))
                for hf in range(2):
                    m_prev = mprev_ref[ci, hf, pl.ds(w, 1), :][:, :1]
                    g_max = gmax_ref[ci, hf, pl.ds(w, 1), :][:, :1]
                    own = tri[(d, hf)]
                    b_col = jnp.sum(jnp.where(own, lf_row, 0.0), axis=1, keepdims=True)
                    cg_col = jnp.max(jnp.where(own, g_row, -jnp.inf), axis=1, keepdims=True)
                    m_col = jnp.broadcast_to(jnp.maximum(m_prev, cg_col), (L, W))
                    g_top = jnp.maximum(m_prev, g_max)
                    e_row = jnp.where(half1 == hf, jnp.exp(g_row - g_top), 0.0)
                    ch[hf] = dict(decay=jnp.where(own, jnp.exp(g_row - m_col), 0.0), w_col=jnp.exp(m_prev - m_col),
                                  clamp=jnp.exp(-(b_col + m_col)), keep=jnp.exp(m_prev - g_top),
                                  kte=(kt_win.astype(F32) * e_row).astype(BF16))
                chains.append(ch)
        for ch in chains:
            ch["qk"] = jnp.dot(ch["q_win"], ch["kt_win"], preferred_element_type=F32)
        for hf_i in range(2):
            for ch in chains:
                hf = ch["halves"][hf_i]
                ch[hf]["upd"] = jnp.dot(ch[hf]["kte"], ch["v_aug"], preferred_element_type=F32)
        for ch in chains:
            ch["c_aug"] = c_ref[ch["ci"]]
        for hf_i in range(2):
            for ch in chains:
                hf = ch["halves"][hf_i]
                rows = slice(hf * L, (hf + 1) * L)
                ch[hf]["qc"] = jnp.dot(ch["q_win"][rows], ch["c_aug"].astype(BF16), preferred_element_type=F32)
                ch[hf]["sv"] = jnp.dot((ch["qk"][rows] * ch[hf]["decay"]).astype(BF16), ch["v_aug"],
                                       preferred_element_type=F32)
                ch["c_aug"] = ch[hf]["keep"] * ch["c_aug"] + ch[hf]["upd"]
        for ch in chains:
            for hf in ch["halves"]:
                wgt, qc, sv = ch[hf]["w_col"], ch[hf]["qc"], ch[hf]["sv"]
                num = wgt * qc[:, :hd] + sv[:, :hd]
                den = wgt * qc[:, hd:] + sv[:, hd:]
                ch["h_ref"][pl.ds(ch["r0"] + hf * L, L), ch["cols"]] = (
                    num / jnp.maximum(jnp.abs(den), ch[hf]["clamp"]))
            c_ref[ch["ci"]] = ch["c_aug"]
        return carry

    assert hd == W
    lax.fori_loop(0, n_win, step, 0, unroll=2)

    def merge(t, carry):
        r0 = pl.multiple_of(t * TM, TM)
        hsum = hf_ref[pl.ds(r0, TM), :] + hb_ref[pl.ds(r0, TM), :]
        og = og_ref[0, pl.ds(r0, TM), :].astype(F32)
        ng = ng_ref[...]
        parts = []
        for hh in range(2):
            cols = slice(hh * hd, (hh + 1) * hd)
            parts.append(_rms(hsum[:, cols], ng[:, cols]))
        out_ref[0, pl.ds(r0, TM), :] = (jnp.concatenate(parts, axis=1) * og).astype(BF16)
        return carry

    lax.fori_loop(0, tt // TM, merge, 0)


def _mlstm(q, k_t, v, og, gt, norm_g, *, hd, n_ctx_win):
    bsz, tt, dm = q.shape
    n_pairs = gt.shape[1]
    n_win = gt.shape[2]
    pw = 2 * hd
    nw_pad = -(-n_win // 8) * 8
    tok = pl.BlockSpec((1, tt, pw), lambda b, p: (b, 0, p))
    kern = functools.partial(_mlstm_kernel, hd=hd, n_win=n_win, n_ctx_win=n_ctx_win, tt=tt)
    return pl.pallas_call(
        kern,
        grid=(bsz, n_pairs),
        in_specs=[tok, pl.BlockSpec((1, pw, tt), lambda b, p: (b, p, 0)), tok, tok,
                  pl.BlockSpec((1, 1, n_win, 8, MLSTM_WIN), lambda b, p: (b, p, 0, 0, 0)),
                  pl.BlockSpec((1, pw), lambda b, p: (0, p))],
        out_specs=tok,
        out_shape=jax.ShapeDtypeStruct((bsz, tt, dm), BF16),
        scratch_shapes=[pltpu.VMEM((tt, pw), F32), pltpu.VMEM((tt, pw), F32),
                        pltpu.VMEM((4, hd, 2 * hd), F32), pltpu.VMEM((4, nw_pad, MLSTM_WIN), F32),
                        pltpu.VMEM((4, 2, nw_pad, MLSTM_WIN), F32), pltpu.VMEM((4, 2, nw_pad, MLSTM_WIN), F32),
                        pltpu.VMEM((4, 2, nw_pad, MLSTM_WIN), F32)],
        compiler_params=_cparams(2),
        name="mlstm",
    )(q, k_t, v, og, gt, norm_g)


def _even_out_kernel(u_ref, up_ref, un_ref, dw_ref, lg_ref, lb_ref, mm_ref, w_ref, x_ref, gate_ref, pg_ref,
                     o_ref, sh_ref, *, n_tiles_total, width):
    i = pl.program_id(1)
    pv, nv = _segment_halo_valid(i, n_tiles_total, True)
    dcv = u_ref.shape[2]
    z_mix = jnp.dot(mm_ref[0], w_ref[dcv:, :], preferred_element_type=F32)
    uext = jnp.concatenate([up_ref[0].astype(F32) * pv, u_ref[0].astype(F32), un_ref[0].astype(F32) * nv], axis=0)
    dw = dw_ref[...]
    base = CONV_HALO - width // 2
    span = sh_ref.shape[1]
    for r in range(8):
        sh_ref[r] = uext[base + r:base + r + span]
    acc = jnp.zeros((TM, uext.shape[1]), F32)
    for j in range(width):
        acc = acc + dw[j:j + 1, :] * sh_ref[j % 8, 8 * (j // 8):8 * (j // 8) + TM, :]
    mu = jnp.mean(acc, axis=-1, keepdims=True)
    cen = acc - mu
    var = jnp.mean(cen * cen, axis=-1, keepdims=True)
    y = cen * lax.rsqrt(var + EPS) * lg_ref[...] + lb_ref[...]
    conv_out = (y * jax.nn.sigmoid(y)).astype(BF16)
    z = z_mix + jnp.dot(conv_out, w_ref[:dcv, :], preferred_element_type=F32)
    o_ref[0] = x_ref[0] + gate_ref[0] * _rms(z, pg_ref[...])


def _even_out(u, mm, xx, gate, post_g, dw_w, ln_g, ln_b, w_out, *, n_batch):
    bsz, tt, d = xx.shape
    nt = tt // TM
    dc = u.shape[-1]
    hb = TM // CONV_HALO
    n_hblk = tt // CONV_HALO
    tok = lambda w: pl.BlockSpec((1, TM, w), lambda b, i: (b, i, 0))
    kern = functools.partial(_even_out_kernel, n_tiles_total=nt, width=dw_w.shape[0])
    return pl.pallas_call(
        kern,
        grid=(bsz, nt),
        in_specs=[tok(dc),
                  pl.BlockSpec((1, CONV_HALO, dc), lambda b, i: (b, jnp.maximum(i * hb - 1, 0), 0)),
                  pl.BlockSpec((1, CONV_HALO, dc), lambda b, i: (b, jnp.minimum((i + 1) * hb, n_hblk - 1), 0)),
                  _const_spec(dw_w.shape), _const_spec((1, dc)), _const_spec((1, dc)),
                  tok(mm.shape[-1]), _const_spec(w_out.shape), tok(d), _mod_spec(d, 0, n_batch),
                  _const_spec((1, d))],
        out_specs=tok(d),
        out_shape=jax.ShapeDtypeStruct((bsz, tt, d), F32),
        scratch_shapes=[pltpu.VMEM((8, TM + 8 * ((dw_w.shape[0] - 1) // 8), dc), F32)],
        compiler_params=_cparams(2),
        name="even_out",
    )(u, u, u, dw_w, ln_g, ln_b, mm, w_out, xx, gate, post_g)


def _odd_proj_kernel(x_ref, sh_ref, sc_ref, g_ref, wp_ref, wqk_ref, wv_ref, cos_ref, sin_ref, qg_ref, kg_ref,
                     pool_ref, qt_ref, k_ref, vt_ref, *, hd, n_q, n_kv, q_scale):
    x = x_ref[0]
    h = (_rms(x, g_ref[...]) * (1.0 + sc_ref[0]) + sh_ref[0]).astype(BF16)
    nt_dims = (((1,), (1,)), ((), ()))
    n_heads = n_q + n_kv
    split = n_heads // 2
    pool = jnp.dot(h, wp_ref[...], preferred_element_type=F32)
    qk_parts = [lax.dot_general(wqk_ref[lo * hd:hi * hd, :], h, nt_dims, preferred_element_type=F32)
                for lo, hi in ((0, split), (split, n_heads))]
    v_t = lax.dot_general(wv_ref[...], h, nt_dims, preferred_element_type=F32)
    pool_ref[0] = pool
    cos = cos_ref[...]
    sin = sin_ref[...]
    half = hd // 2
    for hi in range(n_heads):
        part, base = (qk_parts[0], 0) if hi < split else (qk_parts[1], split)
        t = part[(hi - base) * hd:(hi - base + 1) * hd]
        gain = qg_ref[...] if hi < n_q else kg_ref[...]
        tn = t * lax.rsqrt(jnp.mean(t * t, axis=0, keepdims=True) + EPS) * gain
        x1 = tn[:half]
        x2 = tn[half:]
        rot = jnp.concatenate([x1 * cos - x2 * sin, x1 * sin + x2 * cos], axis=0)
        if hi < n_q:
            kvh, gq = divmod(hi, Q_PER_KV)
            qt_ref[0, kvh, 0, :, gq * TM:(gq + 1) * TM] = (rot * q_scale).astype(BF16)
        else:
            k_ref[0, hi - n_q] = rot.T.astype(BF16)
    for kvh in range(n_kv):
        vt_ref[0, kvh] = v_t[kvh * hd:(kvh + 1) * hd].astype(BF16)


def _odd_proj(xx, shift, scale, pre_g, w_pool, w_qk_t, w_v_t, cos_t, sin_t, qg_col, kg_col, *, n_batch, hd, n_q, n_kv):
    bsz, tt, d = xx.shape
    nt = tt // TM
    dp = w_pool.shape[1]
    kern = functools.partial(_odd_proj_kernel, hd=hd, n_q=n_q, n_kv=n_kv, q_scale=float(hd) ** -0.5 * LOG2_E)
    return pl.pallas_call(
        kern,
        grid=(bsz, nt),
        in_specs=[pl.BlockSpec((1, TM, d), lambda b, i: (b, i, 0)),
                  _mod_spec(d, 0, n_batch), _mod_spec(d, 0, n_batch), _const_spec((1, d)),
                  _const_spec(w_pool.shape), _const_spec(w_qk_t.shape), _const_spec(w_v_t.shape),
                  pl.BlockSpec((hd // 2, TM), lambda b, i: (0, i)), pl.BlockSpec((hd // 2, TM), lambda b, i: (0, i)),
                  _const_spec((hd, 1)), _const_spec((hd, 1))],
        out_specs=[pl.BlockSpec((1, TM, dp), lambda b, i: (b, i, 0)),
                   pl.BlockSpec((1, n_kv, 1, hd, Q_PER_KV * TM), lambda b, i: (b, 0, i, 0, 0)),
                   pl.BlockSpec((1, n_kv, TM, hd), lambda b, i: (b, 0, i, 0)),
                   pl.BlockSpec((1, n_kv, hd, TM), lambda b, i: (b, 0, 0, i))],
        out_shape=[jax.ShapeDtypeStruct((bsz, tt, dp), F32),
                   jax.ShapeDtypeStruct((bsz, n_kv, nt, hd, Q_PER_KV * TM), BF16),
                   jax.ShapeDtypeStruct((bsz, n_kv, tt, hd), BF16),
                   jax.ShapeDtypeStruct((bsz, n_kv, hd, tt), BF16)],
        compiler_params=_cparams(2),
        name="odd_proj",
    )(xx, shift, scale, pre_g, w_pool, w_qk_t, w_v_t, cos_t, sin_t, qg_col, kg_col)


def _attn_kernel(*refs, hd, n_q_tiles):
    qt_refs, (k_ref, vt_ref, o_ref) = refs[:n_q_tiles], refs[n_q_tiles:]
    n_kv, n_keys = k_ref.shape[1], k_ref.shape[2]
    heads = [(t, kvh, g) for t in range(n_q_tiles) for kvh in range(n_kv) for g in range(Q_PER_KV)]
    kc = min(ATTN_KEY_CHUNK, n_keys)
    n_kc = n_keys // kc

    def fold8(a, op):
        return op(a.reshape(kc // 8, 8, TM), axis=0)

    def scores(t, kvh, g):
        q_t = qt_refs[t][0, kvh, 0, :, g * TM:(g + 1) * TM]
        chunks, mx = [], None
        for j in range(n_kc):
            s = jnp.dot(k_ref[0, kvh, j * kc:(j + 1) * kc, :], q_t, preferred_element_type=F32)
            chunks.append(s)
            m8 = fold8(s, jnp.max)
            mx = m8 if mx is None else jnp.maximum(mx, m8)
        return chunks, jnp.max(mx, axis=0, keepdims=True)

    nxt = scores(*heads[0])
    for n, (t, kvh, g) in enumerate(heads):
        chunks, mx = nxt
        if n + 1 < len(heads):
            nxt = scores(*heads[n + 1])
        den8, probs = None, []
        for s in chunks:
            p = jnp.exp2(s - mx)
            d8 = fold8(p, jnp.sum)
            den8 = d8 if den8 is None else den8 + d8
            probs.append(p.astype(BF16))
        den = jnp.sum(den8, axis=0, keepdims=True)
        o_t = jnp.dot(vt_ref[0, kvh], jnp.concatenate(probs, axis=0), preferred_element_type=F32)
        col = (kvh * Q_PER_KV + g) * hd
        o_ref[0, t * TM:(t + 1) * TM, col:col + hd] = (o_t / den).T.astype(BF16)


def _attention(q_t, k, v_t, *, first_tile, n_q_tiles, n_keys):
    bsz, n_kv, nt, hd, qw = q_t.shape
    n_steps = (nt - first_tile) // n_q_tiles if n_keys == k.shape[2] else 1
    assert n_keys % min(ATTN_KEY_CHUNK, n_keys) == 0
    q_spec = lambda t: pl.BlockSpec((1, n_kv, 1, hd, qw), lambda b, i: (b, 0, first_tile + i * n_q_tiles + t, 0, 0))
    kern = functools.partial(_attn_kernel, hd=hd, n_q_tiles=n_q_tiles)
    return pl.pallas_call(
        kern,
        grid=(bsz, n_steps),
        in_specs=[q_spec(t) for t in range(n_q_tiles)]
        + [pl.BlockSpec((1, n_kv, n_keys, hd), lambda b, i: (b, 0, 0, 0)),
           pl.BlockSpec((1, n_kv, hd, n_keys), lambda b, i: (b, 0, 0, 0))],
        out_specs=pl.BlockSpec((1, n_q_tiles * TM, n_kv * Q_PER_KV * hd), lambda b, i: (b, i, 0)),
        out_shape=jax.ShapeDtypeStruct((bsz, n_steps * n_q_tiles * TM, n_kv * Q_PER_KV * hd), BF16),
        compiler_params=_cparams(2),
        name="gqa_attention",
    )(*([q_t] * n_q_tiles), k, v_t)


def _odd_out_kernel(u_ref, up_ref, un_ref, pw_ref, ps_ref, at_ref, *rest, n_tiles_total, tile_off, pool_group,
                    seq_latent):
    i = pl.program_id(1)
    j = i + tile_off
    pv, nv = _segment_halo_valid(j, n_tiles_total, True)
    u = u_ref[0]
    dpl = u.shape[1]
    if tile_off == 0:
        atc_ref, w_ref, x_ref, gate_ref, pg_ref, o_ref = rest
        attn = jnp.where(j == 0, atc_ref[0], at_ref[0])
    else:
        w_ref, x_ref, gate_ref, pg_ref, o_ref = rest
        attn = at_ref[0]
    z_attn = jnp.dot(attn, w_ref[dpl:, :], preferred_element_type=F32)
    uext = jnp.concatenate([up_ref[0] * pv, u, un_ref[0] * nv], axis=0)
    n_ext = uext.shape[0]
    s2 = uext[:n_ext - 1] + uext[1:]
    s4 = s2[:n_ext - 3] + s2[2:]
    s8 = s4[:n_ext - 7] + s4[4:]
    s16 = s8[:n_ext - 15] + s8[8:]
    sums = (s2, s4, s8, s16)
    seg_start = jnp.where(j == 0, 0, 1)
    seg_len = jnp.where(j == 0, TM, seq_latent).astype(F32)
    t = ((j - seg_start) * TM + lax.broadcasted_iota(jnp.int32, (TM, 1), 0)).astype(F32)
    lane_group = lax.broadcasted_iota(jnp.int32, (1, u.shape[1]), 1) // pool_group
    mean = jnp.zeros_like(u)
    for gi, w in enumerate(POOL_WINDOWS):
        hw = w // 2
        win = sums[gi][SMALL_HALO - hw:SMALL_HALO - hw + TM]
        cnt = jnp.minimum(t + hw, seg_len) - jnp.maximum(t - hw, 0.0)
        mean = jnp.where(lane_group == gi, win / cnt, mean)
    dpool = (mean - u).astype(BF16)
    y = jnp.dot(dpool, pw_ref[...], preferred_element_type=F32) * ps_ref[...]
    z = z_attn + jnp.dot(y.astype(BF16), w_ref[:dpl, :], preferred_element_type=F32)
    o_ref[0] = x_ref[0] + gate_ref[0] * _rms(z, pg_ref[...])


def _odd_out(pool_u, attn, attn_ctx, xx, gate, post_g, pool_bd, pool_scale, w_out, *, n_batch, tile_off, seq_latent):
    bsz, tt, d = xx.shape
    nt = tt // TM
    n_out = nt - tile_off
    dp = pool_u.shape[-1]
    hb = TM // SMALL_HALO
    n_hblk = tt // SMALL_HALO
    kern = functools.partial(_odd_out_kernel, n_tiles_total=nt, tile_off=tile_off,
                             pool_group=dp // len(POOL_WINDOWS), seq_latent=seq_latent)
    return pl.pallas_call(
        kern,
        grid=(bsz, n_out),
        in_specs=[pl.BlockSpec((1, TM, dp), lambda b, i: (b, i + tile_off, 0)),
                  pl.BlockSpec((1, SMALL_HALO, dp), lambda b, i: (b, jnp.maximum((i + tile_off) * hb - 1, 0), 0)),
                  pl.BlockSpec((1, SMALL_HALO, dp),
                               lambda b, i: (b, jnp.minimum((i + tile_off + 1) * hb, n_hblk - 1), 0)),
                  _const_spec(pool_bd.shape), _const_spec((1, dp)),
                  pl.BlockSpec((1, TM, attn.shape[-1]), lambda b, i: (b, jnp.maximum(i + tile_off - 1, 0), 0))]
        + ([pl.BlockSpec((1, TM, attn.shape[-1]), lambda b, i: (b, 0, 0))] if tile_off == 0 else [])
        + [_const_spec(w_out.shape),
                  pl.BlockSpec((1, TM, d), lambda b, i: (b, i + tile_off, 0)),
                  _mod_spec(d, tile_off, n_batch), _const_spec((1, d))],
        out_specs=pl.BlockSpec((1, TM, d), lambda b, i: (b, i, 0)),
        out_shape=jax.ShapeDtypeStruct((bsz, n_out * TM, d), F32),
        compiler_params=_cparams(2),
        name="odd_out",
    )(pool_u, pool_u, pool_u, pool_bd, pool_scale, attn, *([attn_ctx] if tile_off == 0 else []), w_out, xx, gate,
      post_g)


def _ffn_kernel(x_ref, xp_ref, xn_ref, sha_ref, shb_ref, sca_ref, scb_ref, gta_ref, gtb_ref, g_ref, pg_ref,
                wi_ref, cw_ref, wo_ref, o_ref, *, tiles_per_sample, has_ctx):
    d_ff = wo_ref.shape[0]
    n_chunks = d_ff // FF_CHUNK
    s = pl.program_id(0)
    xs = (x_ref[:TM], x_ref[TM:])
    halos = ((xp_ref[...], xs[1][:SMALL_HALO]), (xs[0][TM - SMALL_HALO:], xn_ref[...]))
    mods = ((sha_ref, sca_ref, gta_ref), (shb_ref, scb_ref, gtb_ref))
    row = lax.broadcasted_iota(jnp.int32, (TM + 2 * SMALL_HALO, 1), 0)

    def normed(sub):
        j = lax.rem(2 * s + sub, tiles_per_sample)
        pv, nv = _segment_halo_valid(j, tiles_per_sample, has_ctx)
        xe = jnp.concatenate([halos[sub][0], xs[sub], halos[sub][1]], axis=0)
        he = _rms(xe, g_ref[...]) * (1.0 + mods[sub][1][0]) + mods[sub][0][0]
        keep = jnp.where(row < SMALL_HALO, pv, jnp.where(row >= SMALL_HALO + TM, nv, 1.0))
        return (he * keep).astype(BF16)

    def up_proj(he, c):
        lo = c * FF_CHUNK
        return (jnp.dot(he, wi_ref[:, lo:lo + FF_CHUNK], preferred_element_type=F32),
                jnp.dot(he[SMALL_HALO:SMALL_HALO + TM], wi_ref[:, d_ff + lo:d_ff + lo + FF_CHUNK],
                        preferred_element_type=F32))

    hes = [normed(0), None]
    items = [(sub, c) for sub in range(2) for c in range(n_chunks)]
    nxt = up_proj(hes[0], 0)
    acc = None
    for n, (sub, c) in enumerate(items):
        g, v = nxt
        if n + 1 < len(items):
            nsub, nc = items[n + 1]
            if hes[nsub] is None:
                hes[nsub] = normed(nsub)
            nxt = up_proj(hes[nsub], nc)
        cw = cw_ref[:, c * FF_CHUNK:(c + 1) * FF_CHUNK]
        gc = (cw[0:1] * g[SMALL_HALO - 1:SMALL_HALO - 1 + TM] + cw[1:2] * g[SMALL_HALO:SMALL_HALO + TM]
              + cw[2:3] * g[SMALL_HALO + 1:SMALL_HALO + 1 + TM])
        u = (gc * jax.nn.sigmoid(gc) * v).astype(BF16)
        part = jnp.dot(u, wo_ref[c * FF_CHUNK:(c + 1) * FF_CHUNK, :], preferred_element_type=F32)
        acc = part if c == 0 else acc + part
        if c == n_chunks - 1:
            o_ref[sub * TM:(sub + 1) * TM, :] = xs[sub] + mods[sub][2][0] * _rms(acc, pg_ref[...])


def _ffn(xx, shift, scale, gate, pre_g, post_g, w_in, conv_w, w_out, *, n_batch, has_ctx):
    bsz, tt, d = xx.shape
    tps = tt // TM
    n_tiles = bsz * tps
    assert n_tiles % 2 == 0
    hb = TM // SMALL_HALO
    n_hblk = bsz * tt // SMALL_HALO

    def mod_spec(sub):
        def index(s):
            g = 2 * s + sub
            b = g // tps
            return (jnp.where(g % tps == 0, n_batch, b) if has_ctx else b, 0, 0)
        return pl.BlockSpec((1, 1, d), index)

    kern = functools.partial(_ffn_kernel, tiles_per_sample=tps, has_ctx=has_ctx)
    xf = xx.reshape(bsz * tt, d)
    out = pl.pallas_call(
        kern,
        grid=(n_tiles // 2,),
        in_specs=[pl.BlockSpec((2 * TM, d), lambda s: (s, 0)),
                  pl.BlockSpec((SMALL_HALO, d), lambda s: (jnp.maximum(2 * s * hb - 1, 0), 0)),
                  pl.BlockSpec((SMALL_HALO, d), lambda s: (jnp.minimum((2 * s + 2) * hb, n_hblk - 1), 0)),
                  mod_spec(0), mod_spec(1), mod_spec(0), mod_spec(1), mod_spec(0), mod_spec(1),
                  _const_spec((1, d)), _const_spec((1, d)),
                  _const_spec(w_in.shape), _const_spec(conv_w.shape), _const_spec(w_out.shape)],
        out_specs=pl.BlockSpec((2 * TM, d), lambda s: (s, 0)),
        out_shape=jax.ShapeDtypeStruct((bsz * tt, d), F32),
        compiler_params=_cparams(1),
        name="conv_ffn",
    )(xf, xf, xf, shift, shift, scale, scale, gate, gate, pre_g, post_g, w_in, conv_w, w_out)
    return out.reshape(bsz, tt, d)


def _rope_tables_t(seq, ctx_len, hd):
    rows = seq // GRID_W
    row = jnp.repeat(jnp.arange(rows), GRID_W).astype(F32)
    col = jnp.tile(jnp.arange(GRID_W), rows).astype(F32)
    n_freq = hd // 4
    inv = ROPE_THETA ** (-jnp.arange(n_freq, dtype=F32) / n_freq)
    ang = jnp.concatenate([row[:, None] * inv, col[:, None] * inv], axis=-1)
    cos = jnp.concatenate([jnp.ones((ctx_len, hd // 2), F32), jnp.cos(ang)], axis=0)
    sin = jnp.concatenate([jnp.zeros((ctx_len, hd // 2), F32), jnp.sin(ang)], axis=0)
    return cos.T, sin.T


def kernel(x, c, ctx, c_ctx, ada_w, ada_b, mix_pre_g, mix_post_g, ffn_pre_g, ffn_post_g, ffn_w_in, ffn_conv_w,
           ffn_w_out, even_w_in, even_w_out, conv_dw_w, conv_ln_g, conv_ln_b, mlstm_gate_b, mlstm_norm_g,
           odd_w_in, odd_w_out, pool_w, pool_scale, q_norm_g, k_norm_g):
    bsz, seq, d = x.shape
    ctx_len = ctx.shape[1]
    depth = ada_w.shape[0]
    assert ctx_len == TM and seq % TM == 0 and seq % GRID_W == 0
    dc = conv_dw_w.shape[2]
    dm = mlstm_norm_g.shape[1]
    n_heads = mlstm_gate_b.shape[2]
    hd_m = dm // n_heads
    assert n_heads % 2 == 0 and even_w_in.shape[2] == 2 * dc + 4 * dm + 4 * n_heads
    hd = q_norm_g.shape[1]
    dp = pool_scale.shape[1]
    n_kv = (odd_w_in.shape[2] - dp - (odd_w_out.shape[1] - dp)) // (2 * hd)
    n_q = (odd_w_out.shape[1] - dp) // hd
    assert n_q == n_kv * Q_PER_KV
    assert ffn_w_out.shape[1] % FF_CHUNK == 0

    pad = (-(bsz + 1)) % 8
    cc = jnp.concatenate([c, c_ctx[None, :], jnp.zeros((pad, d), F32)], axis=0)
    mods = _ada_all(cc, ada_w, ada_b)

    def mod(l, k):
        return mods[l, :bsz + 1, k * d:(k + 1) * d].reshape(bsz + 1, 1, d)

    cos_t, sin_t = _rope_tables_t(seq, ctx_len, hd)
    perm = np.concatenate([np.arange(0, hd, 2), np.arange(1, hd, 2)])

    xx = jnp.concatenate([ctx, x], axis=1)
    row2 = lambda a: a.reshape(1, -1)

    for l in range(depth):
        last = l == depth - 1
        if l % 2 == 0:
            e = l // 2
            w_in = even_w_in[e]
            n_main = 2 * dc + 4 * dm
            w_main = w_in[:, :n_main].astype(BF16)
            wg = w_in[:, n_main:].reshape(d, 4, n_heads // 2, 2)
            w_gate_t = wg.transpose(2, 1, 3, 0).reshape(4 * n_heads, d).astype(BF16)
            gate_b_col = mlstm_gate_b[e].reshape(4, n_heads // 2, 2).transpose(1, 0, 2).reshape(4 * n_heads, 1)
            w_k_t = w_in[:, 2 * dc + dm:2 * dc + 2 * dm].T.astype(BF16)
            u, q, k_t, v, og, gt = _even_proj(xx, mod(l, 0), mod(l, 1), row2(mix_pre_g[l]), w_main, w_k_t, w_gate_t,
                                              gate_b_col, n_batch=bsz, dc=dc, dm=dm, hd=hd_m)
            mm = _mlstm(q, k_t, v, og, gt, row2(mlstm_norm_g[e]), hd=hd_m, n_ctx_win=ctx_len // MLSTM_WIN)
            xx = _even_out(u, mm, xx, mod(l, 2), row2(mix_post_g[l]), conv_dw_w[e], row2(conv_ln_g[e]),
                           row2(conv_ln_b[e]), even_w_out[e].astype(BF16), n_batch=bsz)
        else:
            o = l // 2
            w_in = odd_w_in[o]
            w_pool = w_in[:, :dp].astype(BF16)
            w_qk = w_in[:, dp:dp + (n_q + n_kv) * hd].reshape(d, n_q + n_kv, hd)[:, :, perm]
            w_qk_t = w_qk.reshape(d, -1).T.astype(BF16)
            w_v_t = w_in[:, dp + (n_q + n_kv) * hd:].T.astype(BF16)
            pool_u, q_t, k, v_t = _odd_proj(xx, mod(l, 0), mod(l, 1), row2(mix_pre_g[l]), w_pool, w_qk_t, w_v_t,
                                            cos_t, sin_t, q_norm_g[o][perm].reshape(hd, 1),
                                            k_norm_g[o][perm].reshape(hd, 1), n_batch=bsz, hd=hd, n_q=n_q, n_kv=n_kv)
            tile_off = 1 if last else 0
            attn = _attention(q_t, k, v_t, first_tile=1, n_q_tiles=2, n_keys=k.shape[2])
            attn_ctx = None if last else _attention(q_t, k, v_t, first_tile=0, n_q_tiles=1, n_keys=ctx_len)
            pg = dp // len(POOL_WINDOWS)
            pool_bd = jnp.zeros((dp, dp), F32)
            for gi in range(len(POOL_WINDOWS)):
                pool_bd = pool_bd.at[gi * pg:(gi + 1) * pg, gi * pg:(gi + 1) * pg].set(pool_w[o, gi])
            xx = _odd_out(pool_u, attn, attn_ctx, xx, mod(l, 2), row2(mix_post_g[l]), pool_bd.astype(BF16),
                          row2(pool_scale[o]), odd_w_out[o].astype(BF16), n_batch=bsz, tile_off=tile_off,
                          seq_latent=seq)
        xx = _ffn(xx, mod(l, 3), mod(l, 4), mod(l, 5), row2(ffn_pre_g[l]), row2(ffn_post_g[l]),
                  ffn_w_in[l].astype(BF16), ffn_conv_w[l], ffn_w_out[l].astype(BF16), n_batch=bsz,
                  has_ctx=xx.shape[1] != seq)
    return xx if xx.shape[1] == seq else xx[:, ctx_len:]
```

```python
import functools

import jax
import jax.numpy as jnp
import numpy as np
from jax import lax
from jax.experimental import pallas as pl
from jax.experimental.pallas import tpu as pltpu

F32 = jnp.float32
BF16 = jnp.bfloat16

GRID_W = 64
MLSTM_WIN = 128
POOL_WINDOWS = (2, 4, 8, 16)
Q_PER_KV = 3
ROPE_THETA = 10000.0
EPS = 1e-6
LOG2_E = 1.4426950408889634

TM = 256
CONV_HALO = 16
SMALL_HALO = 8
FF_CHUNK = 256
ATTN_KEY_CHUNK = 256
VMEM_LIMIT = 56 * 1024 * 1024


def _cparams(n_axes, vmem=VMEM_LIMIT):
    return pltpu.CompilerParams(dimension_semantics=("arbitrary",) * n_axes, vmem_limit_bytes=vmem)


def _const_spec(shape):
    nd = len(shape)
    return pl.BlockSpec(shape, lambda *_: (0,) * nd)


def _rms(x, g):
    return x * lax.rsqrt(jnp.mean(x * x, axis=-1, keepdims=True) + EPS) * g


def _segment_halo_valid(j, n_tiles_total, has_ctx):
    if has_ctx:
        prev_ok = jnp.logical_and(j != 0, j != 1)
        next_ok = jnp.logical_and(j != 0, j != n_tiles_total - 1)
    else:
        prev_ok = j != 0
        next_ok = j != n_tiles_total - 1
    return prev_ok.astype(F32), next_ok.astype(F32)


def _ada_kernel(c_ref, w_ref, b_ref, o_ref):
    c = c_ref[...]
    s = c * jax.nn.sigmoid(c)
    o_ref[0] = jnp.dot(s, w_ref[0], preferred_element_type=F32, precision=lax.Precision.HIGHEST) + b_ref[0]


def _ada_all(cc, ada_w, ada_b):
    depth, d, n = ada_w.shape
    nb = n // 4
    rows = cc.shape[0]
    return pl.pallas_call(
        _ada_kernel,
        grid=(depth, n // nb),
        in_specs=[pl.BlockSpec((rows, d), lambda l, j: (0, 0)),
                  pl.BlockSpec((1, d, nb), lambda l, j: (l, 0, j)),
                  pl.BlockSpec((1, 1, nb), lambda l, j: (l, 0, j))],
        out_specs=pl.BlockSpec((1, rows, nb), lambda l, j: (l, 0, j)),
        out_shape=jax.ShapeDtypeStruct((depth, rows, n), F32),
        compiler_params=_cparams(2),
        name="ada_mod",
    )(cc, ada_w, ada_b.reshape(depth, 1, n))


def _stream_specs(stream, d):
    if isinstance(stream, tuple):
        return [pl.BlockSpec((1, TM, d), lambda b, i: (b, 0, 0)),
                pl.BlockSpec((1, TM, d), lambda b, i: (b, jnp.maximum(i - 1, 0), 0))]
    return [pl.BlockSpec((1, TM, d), lambda b, i: (b, i, 0))]


def _stream_tile(refs, i):
    if len(refs) == 2:
        return jnp.where(i == 0, refs[0][0], refs[1][0])
    return refs[0][0]


def _mod_spec(d, off, n_batch):
    if off == 0:
        return pl.BlockSpec((1, 1, d), lambda b, i: (jnp.where(i == 0, n_batch, b), 0, 0))
    return pl.BlockSpec((1, 1, d), lambda b, i: (b, 0, 0))


def _even_proj_kernel(*refs, n_x, dc, dm, k_scale):
    (sh_ref, sc_ref, g_ref, w_ref, wk_ref, wg_ref, gb_ref,
     u_ref, q_ref, kt_ref, v_ref, o_ref, gt_ref) = refs[n_x:]
    x = _stream_tile(refs[:n_x], pl.program_id(1))
    h = (_rms(x, g_ref[...]) * (1.0 + sc_ref[0]) + sh_ref[0]).astype(BF16)
    nt_dims = (((1,), (1,)), ((), ()))

    def proj(lo, hi):
        return jnp.dot(h, w_ref[:, lo:hi], preferred_element_type=F32)

    off = 2 * dc
    p_glu = proj(0, off)
    p_q = proj(off, off + dm)
    u_ref[0] = (p_glu[:, :dc] * jax.nn.sigmoid(p_glu[:, dc:])).astype(BF16)
    k_t = lax.dot_general(wk_ref[...], h, nt_dims, preferred_element_type=F32)
    q_ref[0] = p_q.astype(BF16)
    p_v = proj(off + 2 * dm, off + 3 * dm)
    kt_ref[0] = (k_t * k_scale).astype(BF16)
    p_o = proj(off + 3 * dm, off + 4 * dm)
    v_ref[0] = p_v.astype(BF16)
    gt = lax.dot_general(wg_ref[...], h, nt_dims, preferred_element_type=F32) + gb_ref[...]
    o_ref[0] = jax.nn.sigmoid(p_o).astype(BF16)
    row = lax.broadcasted_iota(jnp.int32, gt.shape, 0)
    gt = jnp.where((row & 2) != 0, jax.nn.log_sigmoid(gt), gt)
    for pr in range(gt.shape[0] // 8):
        for c in range(TM // MLSTM_WIN):
            gt_ref[0, pr, c] = gt[pr * 8:(pr + 1) * 8, c * MLSTM_WIN:(c + 1) * MLSTM_WIN]


def _even_proj(stream, shift, scale, pre_g, w_main, w_k_t, w_gate_t, gate_b_col, *, n_batch, dc, dm, hd):
    xs = stream if isinstance(stream, tuple) else (stream,)
    bsz, d = xs[0].shape[0], xs[0].shape[2]
    tt = sum(a.shape[1] for a in xs)
    nt = tt // TM
    n_sub = TM // MLSTM_WIN
    n_pairs = w_gate_t.shape[0] // 8
    n_win = tt // MLSTM_WIN
    tok = lambda w: pl.BlockSpec((1, TM, w), lambda b, i: (b, i, 0))
    kern = functools.partial(_even_proj_kernel, n_x=len(xs), dc=dc, dm=dm, k_scale=float(hd) ** -0.5)
    return pl.pallas_call(
        kern,
        grid=(bsz, nt),
        in_specs=_stream_specs(stream, d) + [_mod_spec(d, 0, n_batch), _mod_spec(d, 0, n_batch), _const_spec((1, d)),
                  _const_spec(w_main.shape), _const_spec(w_k_t.shape), _const_spec(w_gate_t.shape),
                  _const_spec(gate_b_col.shape)],
        out_specs=[tok(dc), tok(dm), pl.BlockSpec((1, dm, TM), lambda b, i: (b, 0, i)), tok(dm), tok(dm),
                   pl.BlockSpec((1, n_pairs, n_sub, 8, MLSTM_WIN), lambda b, i: (b, 0, i, 0, 0))],
        out_shape=[jax.ShapeDtypeStruct((bsz, tt, dc), BF16), jax.ShapeDtypeStruct((bsz, tt, dm), BF16),
                   jax.ShapeDtypeStruct((bsz, dm, tt), BF16), jax.ShapeDtypeStruct((bsz, tt, dm), BF16),
                   jax.ShapeDtypeStruct((bsz, tt, dm), BF16),
                   jax.ShapeDtypeStruct((bsz, n_pairs, n_win, 8, MLSTM_WIN), F32)],
        compiler_params=_cparams(2),
        name="even_proj",
    )(*xs, shift, scale, pre_g, w_main, w_k_t, w_gate_t, gate_b_col)


def _mlstm_kernel(q_ref, kt_ref, v_ref, og_ref, gt_ref, ng_ref, out_ref,
                  hf_ref, hb_ref, c_ref, grow_ref, gmax_ref, blast_ref, mprev_ref,
                  *, hd, n_win, n_ctx_win, tt):
    W = MLSTM_WIN
    rowi = lax.broadcasted_iota(jnp.int32, (W, W), 0)
    coli = lax.broadcasted_iota(jnp.int32, (W, W), 1)
    tri_by_dir = (coli <= rowi, coli >= rowi)
    ones_blk = jnp.ones((W, hd), BF16)

    def win_of(d, t):
        if d == 0:
            return t
        return jnp.where(t < n_ctx_win, n_ctx_win - 1 - t, n_win - 1 - (t - n_ctx_win))

    for d in range(2):
        cum = jnp.where(rowi <= coli if d == 0 else rowi >= coli, 1.0, 0.0)
        for hh in range(2):
            ci = d * 2 + hh
            i_rows = gt_ref[0, 0, :, 4 * d + hh, :]
            lf_rows = gt_ref[0, 0, :, 4 * d + 2 + hh, :]
            b_rows = jnp.dot(lf_rows, cum, preferred_element_type=F32, precision=lax.Precision.HIGHEST)
            g_rows = i_rows - b_rows
            grow_ref[ci, :n_win, :] = g_rows
            gmax_ref[ci, :n_win, :] = jnp.broadcast_to(jnp.max(g_rows, axis=1, keepdims=True), (n_win, W))
            blast_ref[ci, :n_win, :] = jnp.broadcast_to(jnp.sum(lf_rows, axis=1, keepdims=True), (n_win, W))

    def scan(t, ms):
        out = []
        for d in range(2):
            w = win_of(d, t)
            for hh in range(2):
                ci = d * 2 + hh
                mprev_ref[ci, pl.ds(w, 1), :] = ms[ci]
                out.append(blast_ref[ci, pl.ds(w, 1), :] + jnp.maximum(ms[ci], gmax_ref[ci, pl.ds(w, 1), :]))
        return tuple(out)

    lax.fori_loop(0, n_win, scan, (jnp.zeros((1, W), F32),) * 4)

    c_ref[...] = jnp.zeros(c_ref.shape, F32)

    def step(t, carry):
        chains = []
        for d in range(2):
            w = win_of(d, t)
            r0 = pl.multiple_of(w * W, W)
            gt = gt_ref[0, 0, w]
            tri = tri_by_dir[d]
            for hh in range(2):
                ci = d * 2 + hh
                cols = slice(hh * hd, (hh + 1) * hd)
                lf_row = gt[4 * d + 2 + hh:4 * d + 3 + hh, :]
                g_row = grow_ref[ci, pl.ds(w, 1), :]
                m_prev = mprev_ref[ci, pl.ds(w, 1), :][:, :1]
                g_max = gmax_ref[ci, pl.ds(w, 1), :][:, :1]
                b_col = jnp.sum(jnp.where(tri, lf_row, 0.0), axis=1, keepdims=True)
                cg_col = jnp.max(jnp.where(tri, g_row, -jnp.inf), axis=1, keepdims=True)
                m_col = jnp.broadcast_to(jnp.maximum(m_prev, cg_col), (W, W))
                g_top = jnp.maximum(m_prev, g_max)
                kt_win = kt_ref[0, cols, pl.ds(r0, W)]
                chains.append(dict(
                    ci=ci, r0=r0, cols=cols, h_ref=hf_ref if d == 0 else hb_ref,
                    q_win=q_ref[0, pl.ds(r0, W), cols], kt_win=kt_win,
                    v_aug=jnp.concatenate([v_ref[0, pl.ds(r0, W), cols], ones_blk], axis=1),
                    decay=jnp.where(tri, jnp.exp(g_row - m_col), 0.0), w_col=jnp.exp(m_prev - m_col),
                    clamp=jnp.exp(-(b_col + m_col)), keep=jnp.exp(m_prev - g_top),
                    kte=(kt_win.astype(F32) * jnp.exp(g_row - g_top)).astype(BF16)))
        for ch in chains:
            ch["qk"] = jnp.dot(ch["q_win"], ch["kt_win"], preferred_element_type=F32)
        for ch in chains:
            ch["c_aug"] = c_ref[ch["ci"]]
            ch["qc"] = jnp.dot(ch["q_win"], ch["c_aug"].astype(BF16), preferred_element_type=F32)
        for ch in chains:
            ch["upd"] = jnp.dot(ch["kte"], ch["v_aug"], preferred_element_type=F32)
        for ch in chains:
            ch["sv"] = jnp.dot((ch["qk"] * ch["decay"]).astype(BF16), ch["v_aug"], preferred_element_type=F32)
        for ch in chains:
            wgt, qc, sv = ch["w_col"], ch["qc"], ch["sv"]
            num = wgt * qc[:, :hd] + sv[:, :hd]
            den = wgt * qc[:, hd:] + sv[:, hd:]
            ch["h_ref"][pl.ds(ch["r0"], W), ch["cols"]] = num / jnp.maximum(jnp.abs(den), ch["clamp"])
            c_ref[ch["ci"]] = ch["keep"] * ch["c_aug"] + ch["upd"]
        return carry

    assert hd == W
    lax.fori_loop(0, n_win, step, 0, unroll=2)

    def merge(t, carry):
        r0 = pl.multiple_of(t * TM, TM)
        hsum = hf_ref[pl.ds(r0, TM), :] + hb_ref[pl.ds(r0, TM), :]
        og = og_ref[0, pl.ds(r0, TM), :].astype(F32)
        ng = ng_ref[...]
        parts = []
        for hh in range(2):
            cols = slice(hh * hd, (hh + 1) * hd)
            parts.append(_rms(hsum[:, cols], ng[:, cols]))
        out_ref[0, pl.ds(r0, TM), :] = (jnp.concatenate(parts, axis=1) * og).astype(BF16)
        return carry

    lax.fori_loop(0, tt // TM, merge, 0)


def _mlstm(q, k_t, v, og, gt, norm_g, *, hd, n_ctx_win):
    bsz, tt, dm = q.shape
    n_pairs = gt.shape[1]
    n_win = gt.shape[2]
    pw = 2 * hd
    nw_pad = -(-n_win // 8) * 8
    tok = pl.BlockSpec((1, tt, pw), lambda b, p: (b, 0, p))
    kern = functools.partial(_mlstm_kernel, hd=hd, n_win=n_win, n_ctx_win=n_ctx_win, tt=tt)
    return pl.pallas_call(
        kern,
        grid=(bsz, n_pairs),
        in_specs=[tok, pl.BlockSpec((1, pw, tt), lambda b, p: (b, p, 0)), tok, tok,
                  pl.BlockSpec((1, 1, n_win, 8, MLSTM_WIN), lambda b, p: (b, p, 0, 0, 0)),
                  pl.BlockSpec((1, pw), lambda b, p: (0, p))],
        out_specs=tok,
        out_shape=jax.ShapeDtypeStruct((bsz, tt, dm), BF16),
        scratch_shapes=[pltpu.VMEM((tt, pw), F32), pltpu.VMEM((tt, pw), F32),
                        pltpu.VMEM((4, hd, 2 * hd), F32), pltpu.VMEM((4, nw_pad, MLSTM_WIN), F32),
                        pltpu.VMEM((4, nw_pad, MLSTM_WIN), F32), pltpu.VMEM((4, nw_pad, MLSTM_WIN), F32),
                        pltpu.VMEM((4, nw_pad, MLSTM_WIN), F32)],
        compiler_params=_cparams(2),
        name="mlstm",
    )(q, k_t, v, og, gt, norm_g)


def _even_out_kernel(u_ref, up_ref, un_ref, dw_ref, lg_ref, lb_ref, mm_ref, w_ref, *rest, n_tiles_total, width):
    gate_ref, pg_ref, o_ref, sh_ref = rest[-4:]
    i = pl.program_id(1)
    pv, nv = _segment_halo_valid(i, n_tiles_total, True)
    dcv = u_ref.shape[2]
    z_mix = jnp.dot(mm_ref[0], w_ref[dcv:, :], preferred_element_type=F32)
    uext = jnp.concatenate([up_ref[0].astype(F32) * pv, u_ref[0].astype(F32), un_ref[0].astype(F32) * nv], axis=0)
    dw = dw_ref[...]
    base = CONV_HALO - width // 2
    span = sh_ref.shape[1]
    for r in range(8):
        sh_ref[r] = uext[base + r:base + r + span]
    acc = jnp.zeros((TM, uext.shape[1]), F32)
    for j in range(width):
        acc = acc + dw[j:j + 1, :] * sh_ref[j % 8, 8 * (j // 8):8 * (j // 8) + TM, :]
    mu = jnp.mean(acc, axis=-1, keepdims=True)
    cen = acc - mu
    var = jnp.mean(cen * cen, axis=-1, keepdims=True)
    y = cen * lax.rsqrt(var + EPS) * lg_ref[...] + lb_ref[...]
    conv_out = (y * jax.nn.sigmoid(y)).astype(BF16)
    z = z_mix + jnp.dot(conv_out, w_ref[:dcv, :], preferred_element_type=F32)
    o_ref[0] = _stream_tile(rest[:-4], i) + gate_ref[0] * _rms(z, pg_ref[...])


def _even_out(u, mm, stream, gate, post_g, dw_w, ln_g, ln_b, w_out, *, n_batch):
    xs = stream if isinstance(stream, tuple) else (stream,)
    bsz, tt, d = u.shape[0], u.shape[1], xs[0].shape[2]
    nt = tt // TM
    dc = u.shape[-1]
    hb = TM // CONV_HALO
    n_hblk = tt // CONV_HALO
    tok = lambda w: pl.BlockSpec((1, TM, w), lambda b, i: (b, i, 0))
    kern = functools.partial(_even_out_kernel, n_tiles_total=nt, width=dw_w.shape[0])
    return pl.pallas_call(
        kern,
        grid=(bsz, nt),
        in_specs=[tok(dc),
                  pl.BlockSpec((1, CONV_HALO, dc), lambda b, i: (b, jnp.maximum(i * hb - 1, 0), 0)),
                  pl.BlockSpec((1, CONV_HALO, dc), lambda b, i: (b, jnp.minimum((i + 1) * hb, n_hblk - 1), 0)),
                  _const_spec(dw_w.shape), _const_spec((1, dc)), _const_spec((1, dc)),
                  tok(mm.shape[-1]), _const_spec(w_out.shape)]
        + _stream_specs(stream, d) + [_mod_spec(d, 0, n_batch), _const_spec((1, d))],
        out_specs=tok(d),
        out_shape=jax.ShapeDtypeStruct((bsz, tt, d), F32),
        scratch_shapes=[pltpu.VMEM((8, TM + 8 * ((dw_w.shape[0] - 1) // 8), dc), F32)],
        compiler_params=_cparams(2),
        name="even_out",
    )(u, u, u, dw_w, ln_g, ln_b, mm, w_out, *xs, gate, post_g)


def _odd_proj_kernel(x_ref, sh_ref, sc_ref, g_ref, wp_ref, wqk_ref, wv_ref, cos_ref, sin_ref, qg_ref, kg_ref,
                     pool_ref, qt_ref, k_ref, vt_ref, *, hd, n_q, n_kv, q_scale):
    x = x_ref[0]
    h = (_rms(x, g_ref[...]) * (1.0 + sc_ref[0]) + sh_ref[0]).astype(BF16)
    nt_dims = (((1,), (1,)), ((), ()))
    n_heads = n_q + n_kv
    split = n_heads // 2
    pool = jnp.dot(h, wp_ref[...], preferred_element_type=F32)
    qk_parts = [lax.dot_general(wqk_ref[lo * hd:hi * hd, :], h, nt_dims, preferred_element_type=F32)
                for lo, hi in ((0, split), (split, n_heads))]
    v_t = lax.dot_general(wv_ref[...], h, nt_dims, preferred_element_type=F32)
    pool_ref[0] = pool
    cos = cos_ref[...]
    sin = sin_ref[...]
    half = hd // 2
    for hi in range(n_heads):
        part, base = (qk_parts[0], 0) if hi < split else (qk_parts[1], split)
        t = part[(hi - base) * hd:(hi - base + 1) * hd]
        gain = qg_ref[...] if hi < n_q else kg_ref[...]
        tn = t * lax.rsqrt(jnp.mean(t * t, axis=0, keepdims=True) + EPS) * gain
        x1 = tn[:half]
        x2 = tn[half:]
        rot = jnp.concatenate([x1 * cos - x2 * sin, x1 * sin + x2 * cos], axis=0)
        if hi < n_q:
            kvh, gq = divmod(hi, Q_PER_KV)
            qt_ref[0, kvh, 0, :, gq * TM:(gq + 1) * TM] = (rot * q_scale).astype(BF16)
        else:
            k_ref[0, hi - n_q] = rot.T.astype(BF16)
    for kvh in range(n_kv):
        vt_ref[0, kvh] = v_t[kvh * hd:(kvh + 1) * hd].astype(BF16)


def _odd_proj(xx, shift, scale, pre_g, w_pool, w_qk_t, w_v_t, cos_t, sin_t, qg_col, kg_col, *, n_batch, hd, n_q, n_kv):
    bsz, tt, d = xx.shape
    nt = tt // TM
    dp = w_pool.shape[1]
    kern = functools.partial(_odd_proj_kernel, hd=hd, n_q=n_q, n_kv=n_kv, q_scale=float(hd) ** -0.5 * LOG2_E)
    return pl.pallas_call(
        kern,
        grid=(bsz, nt),
        in_specs=[pl.BlockSpec((1, TM, d), lambda b, i: (b, i, 0)),
                  _mod_spec(d, 0, n_batch), _mod_spec(d, 0, n_batch), _const_spec((1, d)),
                  _const_spec(w_pool.shape), _const_spec(w_qk_t.shape), _const_spec(w_v_t.shape),
                  pl.BlockSpec((hd // 2, TM), lambda b, i: (0, i)), pl.BlockSpec((hd // 2, TM), lambda b, i: (0, i)),
                  _const_spec((hd, 1)), _const_spec((hd, 1))],
        out_specs=[pl.BlockSpec((1, TM, dp), lambda b, i: (b, i, 0)),
                   pl.BlockSpec((1, n_kv, 1, hd, Q_PER_KV * TM), lambda b, i: (b, 0, i, 0, 0)),
                   pl.BlockSpec((1, n_kv, TM, hd), lambda b, i: (b, 0, i, 0)),
                   pl.BlockSpec((1, n_kv, hd, TM), lambda b, i: (b, 0, 0, i))],
        out_shape=[jax.ShapeDtypeStruct((bsz, tt, dp), F32),
                   jax.ShapeDtypeStruct((bsz, n_kv, nt, hd, Q_PER_KV * TM), BF16),
                   jax.ShapeDtypeStruct((bsz, n_kv, tt, hd), BF16),
                   jax.ShapeDtypeStruct((bsz, n_kv, hd, tt), BF16)],
        compiler_params=_cparams(2),
        name="odd_proj",
    )(xx, shift, scale, pre_g, w_pool, w_qk_t, w_v_t, cos_t, sin_t, qg_col, kg_col)


def _attn_kernel(*refs, hd, n_q_tiles):
    qt_refs, (k_ref, vt_ref, o_ref) = refs[:n_q_tiles], refs[n_q_tiles:]
    n_kv, n_keys = k_ref.shape[1], k_ref.shape[2]
    heads = [(t, kvh, g) for t in range(n_q_tiles) for kvh in range(n_kv) for g in range(Q_PER_KV)]
    kc = min(ATTN_KEY_CHUNK, n_keys)
    n_kc = n_keys // kc

    def fold8(a, op):
        return op(a.reshape(kc // 8, 8, TM), axis=0)

    def scores(t, kvh, g):
        q_t = qt_refs[t][0, kvh, 0, :, g * TM:(g + 1) * TM]
        chunks, mx = [], None
        for j in range(n_kc):
            s = jnp.dot(k_ref[0, kvh, j * kc:(j + 1) * kc, :], q_t, preferred_element_type=F32)
            chunks.append(s)
            m8 = fold8(s, jnp.max)
            mx = m8 if mx is None else jnp.maximum(mx, m8)
        return chunks, jnp.max(mx, axis=0, keepdims=True)

    nxt = scores(*heads[0])
    for n, (t, kvh, g) in enumerate(heads):
        chunks, mx = nxt
        if n + 1 < len(heads):
            nxt = scores(*heads[n + 1])
        den8, probs = None, []
        for s in chunks:
            p = jnp.exp2(s - mx)
            d8 = fold8(p, jnp.sum)
            den8 = d8 if den8 is None else den8 + d8
            probs.append(p.astype(BF16))
        den = jnp.sum(den8, axis=0, keepdims=True)
        o_t = jnp.dot(vt_ref[0, kvh], jnp.concatenate(probs, axis=0), preferred_element_type=F32)
        col = (kvh * Q_PER_KV + g) * hd
        o_ref[0, t * TM:(t + 1) * TM, col:col + hd] = (o_t / den).T.astype(BF16)


def _attention(q_t, k, v_t, *, first_tile, n_q_tiles, n_keys):
    bsz, n_kv, nt, hd, qw = q_t.shape
    n_steps = (nt - first_tile) // n_q_tiles if n_keys == k.shape[2] else 1
    assert n_keys % min(ATTN_KEY_CHUNK, n_keys) == 0
    q_spec = lambda t: pl.BlockSpec((1, n_kv, 1, hd, qw), lambda b, i: (b, 0, first_tile + i * n_q_tiles + t, 0, 0))
    kern = functools.partial(_attn_kernel, hd=hd, n_q_tiles=n_q_tiles)
    return pl.pallas_call(
        kern,
        grid=(bsz, n_steps),
        in_specs=[q_spec(t) for t in range(n_q_tiles)]
        + [pl.BlockSpec((1, n_kv, n_keys, hd), lambda b, i: (b, 0, 0, 0)),
           pl.BlockSpec((1, n_kv, hd, n_keys), lambda b, i: (b, 0, 0, 0))],
        out_specs=pl.BlockSpec((1, n_q_tiles * TM, n_kv * Q_PER_KV * hd), lambda b, i: (b, i, 0)),
        out_shape=jax.ShapeDtypeStruct((bsz, n_steps * n_q_tiles * TM, n_kv * Q_PER_KV * hd), BF16),
        compiler_params=_cparams(2),
        name="gqa_attention",
    )(*([q_t] * n_q_tiles), k, v_t)


def _odd_out_kernel(u_ref, up_ref, un_ref, pw_ref, ps_ref, at_ref, *rest, n_tiles_total, tile_off, pool_group,
                    seq_latent):
    i = pl.program_id(1)
    j = i + tile_off
    pv, nv = _segment_halo_valid(j, n_tiles_total, True)
    u = u_ref[0]
    dpl = u.shape[1]
    if tile_off == 0:
        atc_ref, w_ref, x_ref, gate_ref, pg_ref, o_ref = rest
        attn = jnp.where(j == 0, atc_ref[0], at_ref[0])
    else:
        w_ref, x_ref, gate_ref, pg_ref, o_ref = rest
        attn = at_ref[0]
    z_attn = jnp.dot(attn, w_ref[dpl:, :], preferred_element_type=F32)
    uext = jnp.concatenate([up_ref[0] * pv, u, un_ref[0] * nv], axis=0)
    n_ext = uext.shape[0]
    s2 = uext[:n_ext - 1] + uext[1:]
    s4 = s2[:n_ext - 3] + s2[2:]
    s8 = s4[:n_ext - 7] + s4[4:]
    s16 = s8[:n_ext - 15] + s8[8:]
    sums = (s2, s4, s8, s16)
    seg_start = jnp.where(j == 0, 0, 1)
    seg_len = jnp.where(j == 0, TM, seq_latent).astype(F32)
    t = ((j - seg_start) * TM + lax.broadcasted_iota(jnp.int32, (TM, 1), 0)).astype(F32)
    lane_group = lax.broadcasted_iota(jnp.int32, (1, u.shape[1]), 1) // pool_group
    mean = jnp.zeros_like(u)
    for gi, w in enumerate(POOL_WINDOWS):
        hw = w // 2
        win = sums[gi][SMALL_HALO - hw:SMALL_HALO - hw + TM]
        cnt = jnp.minimum(t + hw, seg_len) - jnp.maximum(t - hw, 0.0)
        mean = jnp.where(lane_group == gi, win / cnt, mean)
    dpool = (mean - u).astype(BF16)
    y = jnp.dot(dpool, pw_ref[...], preferred_element_type=F32) * ps_ref[...]
    z = z_attn + jnp.dot(y.astype(BF16), w_ref[:dpl, :], preferred_element_type=F32)
    o_ref[0] = x_ref[0] + gate_ref[0] * _rms(z, pg_ref[...])


def _odd_out(pool_u, attn, attn_ctx, xx, gate, post_g, pool_bd, pool_scale, w_out, *, n_batch, tile_off, seq_latent):
    bsz, tt, d = xx.shape
    nt = tt // TM
    n_out = nt - tile_off
    dp = pool_u.shape[-1]
    hb = TM // SMALL_HALO
    n_hblk = tt // SMALL_HALO
    kern = functools.partial(_odd_out_kernel, n_tiles_total=nt, tile_off=tile_off,
                             pool_group=dp // len(POOL_WINDOWS), seq_latent=seq_latent)
    return pl.pallas_call(
        kern,
        grid=(bsz, n_out),
        in_specs=[pl.BlockSpec((1, TM, dp), lambda b, i: (b, i + tile_off, 0)),
                  pl.BlockSpec((1, SMALL_HALO, dp), lambda b, i: (b, jnp.maximum((i + tile_off) * hb - 1, 0), 0)),
                  pl.BlockSpec((1, SMALL_HALO, dp),
                               lambda b, i: (b, jnp.minimum((i + tile_off + 1) * hb, n_hblk - 1), 0)),
                  _const_spec(pool_bd.shape), _const_spec((1, dp)),
                  pl.BlockSpec((1, TM, attn.shape[-1]), lambda b, i: (b, jnp.maximum(i + tile_off - 1, 0), 0))]
        + ([pl.BlockSpec((1, TM, attn.shape[-1]), lambda b, i: (b, 0, 0))] if tile_off == 0 else [])
        + [_const_spec(w_out.shape),
                  pl.BlockSpec((1, TM, d), lambda b, i: (b, i + tile_off, 0)),
                  _mod_spec(d, tile_off, n_batch), _const_spec((1, d))],
        out_specs=pl.BlockSpec((1, TM, d), lambda b, i: (b, i, 0)),
        out_shape=jax.ShapeDtypeStruct((bsz, n_out * TM, d), F32),
        compiler_params=_cparams(2),
        name="odd_out",
    )(pool_u, pool_u, pool_u, pool_bd, pool_scale, attn, *([attn_ctx] if tile_off == 0 else []), w_out, xx, gate,
      post_g)


def _ffn_kernel(x_ref, xp_ref, xn_ref, sha_ref, shb_ref, sca_ref, scb_ref, gta_ref, gtb_ref, g_ref, pg_ref,
                wi_ref, cw_ref, wo_ref, o_ref, *, tiles_per_sample, has_ctx):
    d_ff = wo_ref.shape[0]
    n_chunks = d_ff // FF_CHUNK
    s = pl.program_id(0)
    xs = (x_ref[:TM], x_ref[TM:])
    halos = ((xp_ref[...], xs[1][:SMALL_HALO]), (xs[0][TM - SMALL_HALO:], xn_ref[...]))
    mods = ((sha_ref, sca_ref, gta_ref), (shb_ref, scb_ref, gtb_ref))
    row = lax.broadcasted_iota(jnp.int32, (TM + 2 * SMALL_HALO, 1), 0)

    def normed(sub):
        j = lax.rem(2 * s + sub, tiles_per_sample)
        pv, nv = _segment_halo_valid(j, tiles_per_sample, has_ctx)
        xe = jnp.concatenate([halos[sub][0], xs[sub], halos[sub][1]], axis=0)
        he = _rms(xe, g_ref[...]) * (1.0 + mods[sub][1][0]) + mods[sub][0][0]
        keep = jnp.where(row < SMALL_HALO, pv, jnp.where(row >= SMALL_HALO + TM, nv, 1.0))
        return (he * keep).astype(BF16)

    def up_proj(he, c):
        lo = c * FF_CHUNK
        return (jnp.dot(he, wi_ref[:, lo:lo + FF_CHUNK], preferred_element_type=F32),
                jnp.dot(he[SMALL_HALO:SMALL_HALO + TM], wi_ref[:, d_ff + lo:d_ff + lo + FF_CHUNK],
                        preferred_element_type=F32))

    hes = [normed(0), None]
    items = [(sub, c) for sub in range(2) for c in range(n_chunks)]
    nxt = up_proj(hes[0], 0)
    acc = None
    for n, (sub, c) in enumerate(items):
        g, v = nxt
        if n + 1 < len(items):
            nsub, nc = items[n + 1]
            if hes[nsub] is None:
                hes[nsub] = normed(nsub)
            nxt = up_proj(hes[nsub], nc)
        cw = cw_ref[:, c * FF_CHUNK:(c + 1) * FF_CHUNK]
        gc = (cw[0:1] * g[SMALL_HALO - 1:SMALL_HALO - 1 + TM] + cw[1:2] * g[SMALL_HALO:SMALL_HALO + TM]
              + cw[2:3] * g[SMALL_HALO + 1:SMALL_HALO + 1 + TM])
        u = (gc * jax.nn.sigmoid(gc) * v).astype(BF16)
        part = jnp.dot(u, wo_ref[c * FF_CHUNK:(c + 1) * FF_CHUNK, :], preferred_element_type=F32)
        acc = part if c == 0 else acc + part
        if c == n_chunks - 1:
            o_ref[sub * TM:(sub + 1) * TM, :] = xs[sub] + mods[sub][2][0] * _rms(acc, pg_ref[...])


def _ffn(xx, shift, scale, gate, pre_g, post_g, w_in, conv_w, w_out, *, n_batch, has_ctx):
    bsz, tt, d = xx.shape
    tps = tt // TM
    n_tiles = bsz * tps
    assert n_tiles % 2 == 0
    hb = TM // SMALL_HALO
    n_hblk = bsz * tt // SMALL_HALO

    def mod_spec(sub):
        def index(s):
            g = 2 * s + sub
            b = g // tps
            return (jnp.where(g % tps == 0, n_batch, b) if has_ctx else b, 0, 0)
        return pl.BlockSpec((1, 1, d), index)

    kern = functools.partial(_ffn_kernel, tiles_per_sample=tps, has_ctx=has_ctx)
    xf = xx.reshape(bsz * tt, d)
    out = pl.pallas_call(
        kern,
        grid=(n_tiles // 2,),
        in_specs=[pl.BlockSpec((2 * TM, d), lambda s: (s, 0)),
                  pl.BlockSpec((SMALL_HALO, d), lambda s: (jnp.maximum(2 * s * hb - 1, 0), 0)),
                  pl.BlockSpec((SMALL_HALO, d), lambda s: (jnp.minimum((2 * s + 2) * hb, n_hblk - 1), 0)),
                  mod_spec(0), mod_spec(1), mod_spec(0), mod_spec(1), mod_spec(0), mod_spec(1),
                  _const_spec((1, d)), _const_spec((1, d)),
                  _const_spec(w_in.shape), _const_spec(conv_w.shape), _const_spec(w_out.shape)],
        out_specs=pl.BlockSpec((2 * TM, d), lambda s: (s, 0)),
        out_shape=jax.ShapeDtypeStruct((bsz * tt, d), F32),
        compiler_params=_cparams(1),
        name="conv_ffn",
    )(xf, xf, xf, shift, shift, scale, scale, gate, gate, pre_g, post_g, w_in, conv_w, w_out)
    return out.reshape(bsz, tt, d)


def _rope_tables_t(seq, ctx_len, hd):
    rows = seq // GRID_W
    row = jnp.repeat(jnp.arange(rows), GRID_W).astype(F32)
    col = jnp.tile(jnp.arange(GRID_W), rows).astype(F32)
    n_freq = hd // 4
    inv = ROPE_THETA ** (-jnp.arange(n_freq, dtype=F32) / n_freq)
    ang = jnp.concatenate([row[:, None] * inv, col[:, None] * inv], axis=-1)
    cos = jnp.concatenate([jnp.ones((ctx_len, hd // 2), F32), jnp.cos(ang)], axis=0)
    sin = jnp.concatenate([jnp.zeros((ctx_len, hd // 2), F32), jnp.sin(ang)], axis=0)
    return cos.T, sin.T


def kernel(x, c, ctx, c_ctx, ada_w, ada_b, mix_pre_g, mix_post_g, ffn_pre_g, ffn_post_g, ffn_w_in, ffn_conv_w,
           ffn_w_out, even_w_in, even_w_out, conv_dw_w, conv_ln_g, conv_ln_b, mlstm_gate_b, mlstm_norm_g,
           odd_w_in, odd_w_out, pool_w, pool_scale, q_norm_g, k_norm_g):
    bsz, seq, d = x.shape
    ctx_len = ctx.shape[1]
    depth = ada_w.shape[0]
    assert ctx_len == TM and seq % TM == 0 and seq % GRID_W == 0 and depth >= 1
    dc = conv_dw_w.shape[2]
    dm = mlstm_norm_g.shape[1]
    n_heads = mlstm_gate_b.shape[2]
    hd_m = dm // n_heads
    assert n_heads % 2 == 0 and even_w_in.shape[2] == 2 * dc + 4 * dm + 4 * n_heads
    hd = q_norm_g.shape[1]
    dp = pool_scale.shape[1]
    n_kv = (odd_w_in.shape[2] - dp - (odd_w_out.shape[1] - dp)) // (2 * hd)
    n_q = (odd_w_out.shape[1] - dp) // hd
    assert n_q == n_kv * Q_PER_KV
    assert ffn_w_out.shape[1] % FF_CHUNK == 0

    pad = (-(bsz + 1)) % 8
    cc = jnp.concatenate([c, c_ctx[None, :], jnp.zeros((pad, d), F32)], axis=0)
    mods = _ada_all(cc, ada_w, ada_b)

    def mod(l, k):
        return mods[l, :bsz + 1, k * d:(k + 1) * d].reshape(bsz + 1, 1, d)

    cos_t, sin_t = _rope_tables_t(seq, ctx_len, hd)
    perm = np.concatenate([np.arange(0, hd, 2), np.arange(1, hd, 2)])

    xx = (ctx, x)
    row2 = lambda a: a.reshape(1, -1)

    for l in range(depth):
        last = l == depth - 1
        if l % 2 == 0:
            e = l // 2
            w_in = even_w_in[e]
            n_main = 2 * dc + 4 * dm
            w_main = w_in[:, :n_main].astype(BF16)
            wg = w_in[:, n_main:].reshape(d, 4, n_heads // 2, 2)
            w_gate_t = wg.transpose(2, 1, 3, 0).reshape(4 * n_heads, d).astype(BF16)
            gate_b_col = mlstm_gate_b[e].reshape(4, n_heads // 2, 2).transpose(1, 0, 2).reshape(4 * n_heads, 1)
            w_k_t = w_in[:, 2 * dc + dm:2 * dc + 2 * dm].T.astype(BF16)
            u, q, k_t, v, og, gt = _even_proj(xx, mod(l, 0), mod(l, 1), row2(mix_pre_g[l]), w_main, w_k_t, w_gate_t,
                                              gate_b_col, n_batch=bsz, dc=dc, dm=dm, hd=hd_m)
            mm = _mlstm(q, k_t, v, og, gt, row2(mlstm_norm_g[e]), hd=hd_m, n_ctx_win=ctx_len // MLSTM_WIN)
            xx = _even_out(u, mm, xx, mod(l, 2), row2(mix_post_g[l]), conv_dw_w[e], row2(conv_ln_g[e]),
                           row2(conv_ln_b[e]), even_w_out[e].astype(BF16), n_batch=bsz)
        else:
            o = l // 2
            w_in = odd_w_in[o]
            w_pool = w_in[:, :dp].astype(BF16)
            w_qk = w_in[:, dp:dp + (n_q + n_kv) * hd].reshape(d, n_q + n_kv, hd)[:, :, perm]
            w_qk_t = w_qk.reshape(d, -1).T.astype(BF16)
            w_v_t = w_in[:, dp + (n_q + n_kv) * hd:].T.astype(BF16)
            pool_u, q_t, k, v_t = _odd_proj(xx, mod(l, 0), mod(l, 1), row2(mix_pre_g[l]), w_pool, w_qk_t, w_v_t,
                                            cos_t, sin_t, q_norm_g[o][perm].reshape(hd, 1),
                                            k_norm_g[o][perm].reshape(hd, 1), n_batch=bsz, hd=hd, n_q=n_q, n_kv=n_kv)
            tile_off = 1 if last else 0
            attn = _attention(q_t, k, v_t, first_tile=1, n_q_tiles=2, n_keys=k.shape[2])
            attn_ctx = None if last else _attention(q_t, k, v_t, first_tile=0, n_q_tiles=1, n_keys=ctx_len)
            pg = dp // len(POOL_WINDOWS)
            pool_bd = jnp.zeros((dp, dp), F32)
            for gi in range(len(POOL_WINDOWS)):
                pool_bd = pool_bd.at[gi * pg:(gi + 1) * pg, gi * pg:(gi + 1) * pg].set(pool_w[o, gi])
            xx = _odd_out(pool_u, attn, attn_ctx, xx, mod(l, 2), row2(mix_post_g[l]), pool_bd.astype(BF16),
                          row2(pool_scale[o]), odd_w_out[o].astype(BF16), n_batch=bsz, tile_off=tile_off,
                          seq_latent=seq)
        xx = _ffn(xx, mod(l, 3), mod(l, 4), mod(l, 5), row2(ffn_pre_g[l]), row2(ffn_post_g[l]),
                  ffn_w_in[l].astype(BF16), ffn_conv_w[l], ffn_w_out[l].astype(BF16), n_batch=bsz,
                  has_ctx=xx.shape[1] != seq)
    return xx if xx.shape[1] == seq else xx[:, ctx_len:]
```

```python
import functools

import jax
import jax.numpy as jnp
import numpy as np
from jax import lax
from jax.experimental import pallas as pl
from jax.experimental.pallas import tpu as pltpu

F32 = jnp.float32
BF16 = jnp.bfloat16

GRID_W = 64
MLSTM_WIN = 128
POOL_WINDOWS = (2, 4, 8, 16)
Q_PER_KV = 3
ROPE_THETA = 10000.0
EPS = 1e-6
LOG2_E = 1.4426950408889634

TM = 256
CONV_HALO = 16
SMALL_HALO = 8
FF_CHUNK = 256
FF_LOOKAHEAD = 2
ATTN_KEY_CHUNK = 256
VMEM_LIMIT = 56 * 1024 * 1024


def _cparams(n_axes, vmem=VMEM_LIMIT):
    return pltpu.CompilerParams(dimension_semantics=("arbitrary",) * n_axes, vmem_limit_bytes=vmem)


def _const_spec(shape):
    nd = len(shape)
    return pl.BlockSpec(shape, lambda *_: (0,) * nd)


def _rms(x, g):
    return x * lax.rsqrt(jnp.mean(x * x, axis=-1, keepdims=True) + EPS) * g


def _segment_halo_valid(j, n_tiles_total, has_ctx):
    if has_ctx:
        prev_ok = jnp.logical_and(j != 0, j != 1)
        next_ok = jnp.logical_and(j != 0, j != n_tiles_total - 1)
    else:
        prev_ok = j != 0
        next_ok = j != n_tiles_total - 1
    return prev_ok.astype(F32), next_ok.astype(F32)


def _ada_kernel(c_ref, w_ref, b_ref, o_ref):
    c = c_ref[...]
    s = c * jax.nn.sigmoid(c)
    o_ref[0] = jnp.dot(s, w_ref[0], preferred_element_type=F32, precision=lax.Precision.HIGHEST) + b_ref[0]


def _ada_all(cc, ada_w, ada_b):
    depth, d, n = ada_w.shape
    nb = n // 4
    rows = cc.shape[0]
    return pl.pallas_call(
        _ada_kernel,
        grid=(depth, n // nb),
        in_specs=[pl.BlockSpec((rows, d), lambda l, j: (0, 0)),
                  pl.BlockSpec((1, d, nb), lambda l, j: (l, 0, j)),
                  pl.BlockSpec((1, 1, nb), lambda l, j: (l, 0, j))],
        out_specs=pl.BlockSpec((1, rows, nb), lambda l, j: (l, 0, j)),
        out_shape=jax.ShapeDtypeStruct((depth, rows, n), F32),
        compiler_params=_cparams(2),
        name="ada_mod",
    )(cc, ada_w, ada_b.reshape(depth, 1, n))


def _stream_specs(stream, d):
    if isinstance(stream, tuple):
        return [pl.BlockSpec((1, TM, d), lambda b, i: (b, 0, 0)),
                pl.BlockSpec((1, TM, d), lambda b, i: (b, jnp.maximum(i - 1, 0), 0))]
    return [pl.BlockSpec((1, TM, d), lambda b, i: (b, i, 0))]


def _stream_tile(refs, i):
    if len(refs) == 2:
        return jnp.where(i == 0, refs[0][0], refs[1][0])
    return refs[0][0]


class _PairTiles:
    def __init__(self, tps_in, tps_out, tile_off, n_batch):
        self.tps_in, self.tps_out, self.tile_off, self.n_batch = tps_in, tps_out, tile_off, n_batch

    def bj(self, s, sub):
        g = 2 * s + sub
        return g // self.tps_out, g % self.tps_out + self.tile_off

    def tile(self, sub, width):
        def index(s):
            b, j = self.bj(s, sub)
            return (b * self.tps_in + j, 0)
        return pl.BlockSpec((TM, width), index)

    def halo(self, sub, width, rows, after):
        per_tile = TM // rows
        last = self.n_batch * self.tps_in * per_tile - 1

        def index(s):
            b, j = self.bj(s, sub)
            t = (b * self.tps_in + j) * per_tile
            return (jnp.minimum(t + per_tile, last) if after else jnp.maximum(t - 1, 0), 0)
        return pl.BlockSpec((rows, width), index)

    def mod(self, sub, d):
        def index(s):
            b, j = self.bj(s, sub)
            return (jnp.where(j == 0, self.n_batch, b), 0, 0)
        return pl.BlockSpec((1, 1, d), index)

    def tile_in_sample(self, s, sub):
        return lax.rem(2 * s + sub, self.tps_out) + self.tile_off


def _mod_spec(d, off, n_batch):
    if off == 0:
        return pl.BlockSpec((1, 1, d), lambda b, i: (jnp.where(i == 0, n_batch, b), 0, 0))
    return pl.BlockSpec((1, 1, d), lambda b, i: (b, 0, 0))


def _even_proj_kernel(*refs, n_x, dc, dm, k_scale):
    (sh_ref, sc_ref, g_ref, w_ref, wk_ref, wg_ref, gb_ref,
     u_ref, q_ref, kt_ref, v_ref, o_ref, gt_ref) = refs[n_x:]
    x = _stream_tile(refs[:n_x], pl.program_id(1))
    h = (_rms(x, g_ref[...]) * (1.0 + sc_ref[0]) + sh_ref[0]).astype(BF16)
    nt_dims = (((1,), (1,)), ((), ()))

    def proj(lo, hi):
        return jnp.dot(h, w_ref[:, lo:hi], preferred_element_type=F32)

    off = 2 * dc
    p_glu = proj(0, off)
    p_q = proj(off, off + dm)
    u_ref[0] = (p_glu[:, :dc] * jax.nn.sigmoid(p_glu[:, dc:])).astype(BF16)
    k_t = lax.dot_general(wk_ref[...], h, nt_dims, preferred_element_type=F32)
    q_ref[0] = p_q.astype(BF16)
    p_v = proj(off + 2 * dm, off + 3 * dm)
    kt_ref[0] = (k_t * k_scale).astype(BF16)
    p_o = proj(off + 3 * dm, off + 4 * dm)
    v_ref[0] = p_v.astype(BF16)
    gt = lax.dot_general(wg_ref[...], h, nt_dims, preferred_element_type=F32) + gb_ref[...]
    o_ref[0] = jax.nn.sigmoid(p_o).astype(BF16)
    row = lax.broadcasted_iota(jnp.int32, gt.shape, 0)
    gt = jnp.where((row & 2) != 0, jax.nn.log_sigmoid(gt), gt)
    for pr in range(gt.shape[0] // 8):
        for c in range(TM // MLSTM_WIN):
            gt_ref[0, pr, c] = gt[pr * 8:(pr + 1) * 8, c * MLSTM_WIN:(c + 1) * MLSTM_WIN]


def _even_proj(stream, shift, scale, pre_g, w_main, w_k_t, w_gate_t, gate_b_col, *, n_batch, dc, dm, hd):
    xs = stream if isinstance(stream, tuple) else (stream,)
    bsz, d = xs[0].shape[0], xs[0].shape[2]
    tt = sum(a.shape[1] for a in xs)
    nt = tt // TM
    n_sub = TM // MLSTM_WIN
    n_pairs = w_gate_t.shape[0] // 8
    n_win = tt // MLSTM_WIN
    tok = lambda w: pl.BlockSpec((1, TM, w), lambda b, i: (b, i, 0))
    kern = functools.partial(_even_proj_kernel, n_x=len(xs), dc=dc, dm=dm, k_scale=float(hd) ** -0.5)
    return pl.pallas_call(
        kern,
        grid=(bsz, nt),
        in_specs=_stream_specs(stream, d) + [_mod_spec(d, 0, n_batch), _mod_spec(d, 0, n_batch), _const_spec((1, d)),
                  _const_spec(w_main.shape), _const_spec(w_k_t.shape), _const_spec(w_gate_t.shape),
                  _const_spec(gate_b_col.shape)],
        out_specs=[tok(dc), tok(dm), pl.BlockSpec((1, dm, TM), lambda b, i: (b, 0, i)), tok(dm), tok(dm),
                   pl.BlockSpec((1, n_pairs, n_sub, 8, MLSTM_WIN), lambda b, i: (b, 0, i, 0, 0))],
        out_shape=[jax.ShapeDtypeStruct((bsz, tt, dc), BF16), jax.ShapeDtypeStruct((bsz, tt, dm), BF16),
                   jax.ShapeDtypeStruct((bsz, dm, tt), BF16), jax.ShapeDtypeStruct((bsz, tt, dm), BF16),
                   jax.ShapeDtypeStruct((bsz, tt, dm), BF16),
                   jax.ShapeDtypeStruct((bsz, n_pairs, n_win, 8, MLSTM_WIN), F32)],
        compiler_params=_cparams(2),
        name="even_proj",
    )(*xs, shift, scale, pre_g, w_main, w_k_t, w_gate_t, gate_b_col)


def _mlstm_kernel(q_ref, kt_ref, v_ref, og_ref, gt_ref, ng_ref, out_ref,
                  hf_ref, hb_ref, c_ref, grow_ref, gmax_ref, blast_ref, mprev_ref,
                  *, hd, n_win, n_ctx_win, tt):
    W = MLSTM_WIN
    rowi = lax.broadcasted_iota(jnp.int32, (W, W), 0)
    coli = lax.broadcasted_iota(jnp.int32, (W, W), 1)
    tri_by_dir = (coli <= rowi, coli >= rowi)
    ones_blk = jnp.ones((W, hd), BF16)

    def win_of(d, t):
        if d == 0:
            return t
        return jnp.where(t < n_ctx_win, n_ctx_win - 1 - t, n_win - 1 - (t - n_ctx_win))

    for d in range(2):
        cum = jnp.where(rowi <= coli if d == 0 else rowi >= coli, 1.0, 0.0)
        for hh in range(2):
            ci = d * 2 + hh
            i_rows = gt_ref[0, 0, :, 4 * d + hh, :]
            lf_rows = gt_ref[0, 0, :, 4 * d + 2 + hh, :]
            b_rows = jnp.dot(lf_rows, cum, preferred_element_type=F32, precision=lax.Precision.HIGHEST)
            g_rows = i_rows - b_rows
            grow_ref[ci, :n_win, :] = g_rows
            gmax_ref[ci, :n_win, :] = jnp.broadcast_to(jnp.max(g_rows, axis=1, keepdims=True), (n_win, W))
            blast_ref[ci, :n_win, :] = jnp.broadcast_to(jnp.sum(lf_rows, axis=1, keepdims=True), (n_win, W))

    def scan(t, ms):
        out = []
        for d in range(2):
            w = win_of(d, t)
            for hh in range(2):
                ci = d * 2 + hh
                mprev_ref[ci, pl.ds(w, 1), :] = ms[ci]
                out.append(blast_ref[ci, pl.ds(w, 1), :] + jnp.maximum(ms[ci], gmax_ref[ci, pl.ds(w, 1), :]))
        return tuple(out)

    lax.fori_loop(0, n_win, scan, (jnp.zeros((1, W), F32),) * 4)

    c_ref[...] = jnp.zeros(c_ref.shape, F32)

    def step(t, carry):
        chains = []
        for d in range(2):
            w = win_of(d, t)
            r0 = pl.multiple_of(w * W, W)
            gt = gt_ref[0, 0, w]
            tri = tri_by_dir[d]
            for hh in range(2):
                ci = d * 2 + hh
                cols = slice(hh * hd, (hh + 1) * hd)
                lf_row = gt[4 * d + 2 + hh:4 * d + 3 + hh, :]
                g_row = grow_ref[ci, pl.ds(w, 1), :]
                m_prev = mprev_ref[ci, pl.ds(w, 1), :][:, :1]
                g_max = gmax_ref[ci, pl.ds(w, 1), :][:, :1]
                b_col = jnp.sum(jnp.where(tri, lf_row, 0.0), axis=1, keepdims=True)
                cg_col = jnp.max(jnp.where(tri, g_row, -jnp.inf), axis=1, keepdims=True)
                m_col = jnp.broadcast_to(jnp.maximum(m_prev, cg_col), (W, W))
                g_top = jnp.maximum(m_prev, g_max)
                kt_win = kt_ref[0, cols, pl.ds(r0, W)]
                chains.append(dict(
                    ci=ci, r0=r0, cols=cols, h_ref=hf_ref if d == 0 else hb_ref,
                    q_win=q_ref[0, pl.ds(r0, W), cols], kt_win=kt_win,
                    v_aug=jnp.concatenate([v_ref[0, pl.ds(r0, W), cols], ones_blk], axis=1),
                    decay=jnp.where(tri, jnp.exp(g_row - m_col), 0.0), w_col=jnp.exp(m_prev - m_col),
                    clamp=jnp.exp(-(b_col + m_col)), keep=jnp.exp(m_prev - g_top),
                    kte=(kt_win.astype(F32) * jnp.exp(g_row - g_top)).astype(BF16)))
        for ch in chains:
            ch["qk"] = jnp.dot(ch["q_win"], ch["kt_win"], preferred_element_type=F32)
        for ch in chains:
            ch["c_aug"] = c_ref[ch["ci"]]
            ch["qc"] = jnp.dot(ch["q_win"], ch["c_aug"].astype(BF16), preferred_element_type=F32)
        for ch in chains:
            ch["upd"] = jnp.dot(ch["kte"], ch["v_aug"], preferred_element_type=F32)
        for ch in chains:
            ch["sv"] = jnp.dot((ch["qk"] * ch["decay"]).astype(BF16), ch["v_aug"], preferred_element_type=F32)
        for ch in chains:
            wgt, qc, sv = ch["w_col"], ch["qc"], ch["sv"]
            num = wgt * qc[:, :hd] + sv[:, :hd]
            den = wgt * qc[:, hd:] + sv[:, hd:]
            ch["h_ref"][pl.ds(ch["r0"], W), ch["cols"]] = num / jnp.maximum(jnp.abs(den), ch["clamp"])
            c_ref[ch["ci"]] = ch["keep"] * ch["c_aug"] + ch["upd"]
        return carry

    assert hd == W
    lax.fori_loop(0, n_win, step, 0, unroll=2)

    def merge(t, carry):
        r0 = pl.multiple_of(t * TM, TM)
        hsum = hf_ref[pl.ds(r0, TM), :] + hb_ref[pl.ds(r0, TM), :]
        og = og_ref[0, pl.ds(r0, TM), :].astype(F32)
        ng = ng_ref[...]
        parts = []
        for hh in range(2):
            cols = slice(hh * hd, (hh + 1) * hd)
            parts.append(_rms(hsum[:, cols], ng[:, cols]))
        out_ref[0, pl.ds(r0, TM), :] = (jnp.concatenate(parts, axis=1) * og).astype(BF16)
        return carry

    lax.fori_loop(0, tt // TM, merge, 0)


def _mlstm(q, k_t, v, og, gt, norm_g, *, hd, n_ctx_win):
    bsz, tt, dm = q.shape
    n_pairs = gt.shape[1]
    n_win = gt.shape[2]
    pw = 2 * hd
    nw_pad = -(-n_win // 8) * 8
    tok = pl.BlockSpec((1, tt, pw), lambda b, p: (b, 0, p))
    kern = functools.partial(_mlstm_kernel, hd=hd, n_win=n_win, n_ctx_win=n_ctx_win, tt=tt)
    return pl.pallas_call(
        kern,
        grid=(bsz, n_pairs),
        in_specs=[tok, pl.BlockSpec((1, pw, tt), lambda b, p: (b, p, 0)), tok, tok,
                  pl.BlockSpec((1, 1, n_win, 8, MLSTM_WIN), lambda b, p: (b, p, 0, 0, 0)),
                  pl.BlockSpec((1, pw), lambda b, p: (0, p))],
        out_specs=tok,
        out_shape=jax.ShapeDtypeStruct((bsz, tt, dm), BF16),
        scratch_shapes=[pltpu.VMEM((tt, pw), F32), pltpu.VMEM((tt, pw), F32),
                        pltpu.VMEM((4, hd, 2 * hd), F32), pltpu.VMEM((4, nw_pad, MLSTM_WIN), F32),
                        pltpu.VMEM((4, nw_pad, MLSTM_WIN), F32), pltpu.VMEM((4, nw_pad, MLSTM_WIN), F32),
                        pltpu.VMEM((4, nw_pad, MLSTM_WIN), F32)],
        compiler_params=_cparams(2),
        name="mlstm",
    )(q, k_t, v, og, gt, norm_g)


def _even_out_kernel(*refs, pairs, n_res, width):
    per = 5 + n_res
    dw_ref, lg_ref, lb_ref, w_ref, pg_ref, o_ref, sh_ref = refs[2 * per:]
    s = pl.program_id(0)
    dw = dw_ref[...]
    base = CONV_HALO - width // 2
    span = sh_ref.shape[2]
    tiles = []
    for sub in range(2):
        u_ref, up_ref, un_ref, mm_ref = refs[sub * per:sub * per + 4]
        res = refs[sub * per + 4:sub * per + 4 + n_res]
        gate_ref = refs[sub * per + 4 + n_res]
        dcv = u_ref.shape[1]
        z_mix = jnp.dot(mm_ref[...], w_ref[dcv:, :], preferred_element_type=F32)
        tiles.append((u_ref, up_ref, un_ref, res, gate_ref, z_mix, dcv))
    for sub, (u_ref, up_ref, un_ref, res, gate_ref, z_mix, dcv) in enumerate(tiles):
        j = pairs.tile_in_sample(s, sub)
        pv, nv = _segment_halo_valid(j, pairs.tps_in, True)
        uext = jnp.concatenate([up_ref[...].astype(F32) * pv, u_ref[...].astype(F32), un_ref[...].astype(F32) * nv],
                               axis=0)
        for r in range(8):
            sh_ref[sub, r] = uext[base + r:base + r + span]
        acc = jnp.zeros((TM, dcv), F32)
        for t in range(width):
            acc = acc + dw[t:t + 1, :] * sh_ref[sub, t % 8, 8 * (t // 8):8 * (t // 8) + TM, :]
        mu = jnp.mean(acc, axis=-1, keepdims=True)
        cen = acc - mu
        var = jnp.mean(cen * cen, axis=-1, keepdims=True)
        y = cen * lax.rsqrt(var + EPS) * lg_ref[...] + lb_ref[...]
        conv_out = (y * jax.nn.sigmoid(y)).astype(BF16)
        z = z_mix + jnp.dot(conv_out, w_ref[:dcv, :], preferred_element_type=F32)
        x_res = jnp.where(j == 0, res[0][0], res[1][0]) if n_res == 2 else res[0][...]
        o_ref[sub * TM:(sub + 1) * TM, :] = x_res + gate_ref[0] * _rms(z, pg_ref[...])


def _even_out(u, mm, stream, gate, post_g, dw_w, ln_g, ln_b, w_out, *, n_batch):
    xs = stream if isinstance(stream, tuple) else (stream,)
    bsz, tt, dc = u.shape
    d = xs[0].shape[2]
    tps = tt // TM
    assert (bsz * tps) % 2 == 0
    pairs = _PairTiles(tps, tps, 0, n_batch)
    flat = lambda a: a.reshape(bsz * tt, a.shape[2])
    uf, mmf = flat(u), flat(mm)
    in_specs, args = [], []
    for sub in range(2):
        in_specs += [pairs.tile(sub, dc), pairs.halo(sub, dc, CONV_HALO, False), pairs.halo(sub, dc, CONV_HALO, True),
                     pairs.tile(sub, mm.shape[2])]
        args += [uf, uf, uf, mmf]
        if len(xs) == 2:
            def ctx_index(s, sub=sub):
                return (pairs.bj(s, sub)[0], 0, 0)

            def lat_index(s, sub=sub):
                b, j = pairs.bj(s, sub)
                return (b, jnp.maximum(j - 1, 0), 0)
            in_specs += [pl.BlockSpec((1, TM, d), ctx_index), pl.BlockSpec((1, TM, d), lat_index)]
            args += list(xs)
        else:
            in_specs.append(pairs.tile(sub, d))
            args.append(flat(xs[0]))
        in_specs.append(pairs.mod(sub, d))
        args.append(gate)
    in_specs += [_const_spec(dw_w.shape), _const_spec((1, dc)), _const_spec((1, dc)), _const_spec(w_out.shape),
                 _const_spec((1, d))]
    args += [dw_w, ln_g, ln_b, w_out, post_g]
    kern = functools.partial(_even_out_kernel, pairs=pairs, n_res=len(xs), width=dw_w.shape[0])
    out = pl.pallas_call(
        kern,
        grid=(bsz * tps // 2,),
        in_specs=in_specs,
        out_specs=pl.BlockSpec((2 * TM, d), lambda s: (s, 0)),
        out_shape=jax.ShapeDtypeStruct((bsz * tt, d), F32),
        scratch_shapes=[pltpu.VMEM((2, 8, TM + 8 * ((dw_w.shape[0] - 1) // 8), dc), F32)],
        compiler_params=_cparams(1),
        name="even_out",
    )(*args)
    return out.reshape(bsz, tt, d)


def _odd_proj_kernel(x_ref, sh_ref, sc_ref, g_ref, wp_ref, wqk_ref, wv_ref, cos_ref, sin_ref, qg_ref, kg_ref,
                     pool_ref, qt_ref, k_ref, vt_ref, *, hd, n_q, n_kv, q_scale):
    x = x_ref[0]
    h = (_rms(x, g_ref[...]) * (1.0 + sc_ref[0]) + sh_ref[0]).astype(BF16)
    nt_dims = (((1,), (1,)), ((), ()))
    n_heads = n_q + n_kv
    split = n_heads // 2
    pool = jnp.dot(h, wp_ref[...], preferred_element_type=F32)
    qk_parts = [lax.dot_general(wqk_ref[lo * hd:hi * hd, :], h, nt_dims, preferred_element_type=F32)
                for lo, hi in ((0, split), (split, n_heads))]
    v_t = lax.dot_general(wv_ref[...], h, nt_dims, preferred_element_type=F32)
    pool_ref[0] = pool
    cos = cos_ref[...]
    sin = sin_ref[...]
    half = hd // 2
    for hi in range(n_heads):
        part, base = (qk_parts[0], 0) if hi < split else (qk_parts[1], split)
        t = part[(hi - base) * hd:(hi - base + 1) * hd]
        gain = qg_ref[...] if hi < n_q else kg_ref[...]
        tn = t * lax.rsqrt(jnp.mean(t * t, axis=0, keepdims=True) + EPS) * gain
        x1 = tn[:half]
        x2 = tn[half:]
        rot = jnp.concatenate([x1 * cos - x2 * sin, x1 * sin + x2 * cos], axis=0)
        if hi < n_q:
            kvh, gq = divmod(hi, Q_PER_KV)
            qt_ref[0, kvh, 0, :, gq * TM:(gq + 1) * TM] = (rot * q_scale).astype(BF16)
        else:
            k_ref[0, hi - n_q] = rot.T.astype(BF16)
    for kvh in range(n_kv):
        vt_ref[0, kvh] = v_t[kvh * hd:(kvh + 1) * hd].astype(BF16)


def _odd_proj(xx, shift, scale, pre_g, w_pool, w_qk_t, w_v_t, cos_t, sin_t, qg_col, kg_col, *, n_batch, hd, n_q, n_kv):
    bsz, tt, d = xx.shape
    nt = tt // TM
    dp = w_pool.shape[1]
    kern = functools.partial(_odd_proj_kernel, hd=hd, n_q=n_q, n_kv=n_kv, q_scale=float(hd) ** -0.5 * LOG2_E)
    return pl.pallas_call(
        kern,
        grid=(bsz, nt),
        in_specs=[pl.BlockSpec((1, TM, d), lambda b, i: (b, i, 0)),
                  _mod_spec(d, 0, n_batch), _mod_spec(d, 0, n_batch), _const_spec((1, d)),
                  _const_spec(w_pool.shape), _const_spec(w_qk_t.shape), _const_spec(w_v_t.shape),
                  pl.BlockSpec((hd // 2, TM), lambda b, i: (0, i)), pl.BlockSpec((hd // 2, TM), lambda b, i: (0, i)),
                  _const_spec((hd, 1)), _const_spec((hd, 1))],
        out_specs=[pl.BlockSpec((1, TM, dp), lambda b, i: (b, i, 0)),
                   pl.BlockSpec((1, n_kv, 1, hd, Q_PER_KV * TM), lambda b, i: (b, 0, i, 0, 0)),
                   pl.BlockSpec((1, n_kv, TM, hd), lambda b, i: (b, 0, i, 0)),
                   pl.BlockSpec((1, n_kv, hd, TM), lambda b, i: (b, 0, 0, i))],
        out_shape=[jax.ShapeDtypeStruct((bsz, tt, dp), F32),
                   jax.ShapeDtypeStruct((bsz, n_kv, nt, hd, Q_PER_KV * TM), BF16),
                   jax.ShapeDtypeStruct((bsz, n_kv, tt, hd), BF16),
                   jax.ShapeDtypeStruct((bsz, n_kv, hd, tt), BF16)],
        compiler_params=_cparams(2),
        name="odd_proj",
    )(xx, shift, scale, pre_g, w_pool, w_qk_t, w_v_t, cos_t, sin_t, qg_col, kg_col)


def _attn_kernel(*refs, hd, n_q_tiles):
    qt_refs, (k_ref, vt_ref, o_ref) = refs[:n_q_tiles], refs[n_q_tiles:]
    n_kv, n_keys = k_ref.shape[1], k_ref.shape[2]
    heads = [(t, kvh, g) for t in range(n_q_tiles) for kvh in range(n_kv) for g in range(Q_PER_KV)]
    kc = min(ATTN_KEY_CHUNK, n_keys)
    n_kc = n_keys // kc

    def fold8(a, op):
        return op(a.reshape(kc // 8, 8, TM), axis=0)

    def scores(t, kvh, g):
        q_t = qt_refs[t][0, kvh, 0, :, g * TM:(g + 1) * TM]
        chunks, mx = [], None
        for j in range(n_kc):
            s = jnp.dot(k_ref[0, kvh, j * kc:(j + 1) * kc, :], q_t, preferred_element_type=F32)
            chunks.append(s)
            m8 = fold8(s, jnp.max)
            mx = m8 if mx is None else jnp.maximum(mx, m8)
        return chunks, jnp.max(mx, axis=0, keepdims=True)

    nxt = scores(*heads[0])
    for n, (t, kvh, g) in enumerate(heads):
        chunks, mx = nxt
        if n + 1 < len(heads):
            nxt = scores(*heads[n + 1])
        den8, probs = None, []
        for s in chunks:
            p = jnp.exp2(s - mx)
            d8 = fold8(p, jnp.sum)
            den8 = d8 if den8 is None else den8 + d8
            probs.append(p.astype(BF16))
        den = jnp.sum(den8, axis=0, keepdims=True)
        o_t = jnp.dot(vt_ref[0, kvh], jnp.concatenate(probs, axis=0), preferred_element_type=F32)
        col = (kvh * Q_PER_KV + g) * hd
        o_ref[0, t * TM:(t + 1) * TM, col:col + hd] = (o_t / den).T.astype(BF16)


def _attention(q_t, k, v_t, *, first_tile, n_q_tiles, n_keys):
    bsz, n_kv, nt, hd, qw = q_t.shape
    n_steps = (nt - first_tile) // n_q_tiles if n_keys == k.shape[2] else 1
    assert n_keys % min(ATTN_KEY_CHUNK, n_keys) == 0
    q_spec = lambda t: pl.BlockSpec((1, n_kv, 1, hd, qw), lambda b, i: (b, 0, first_tile + i * n_q_tiles + t, 0, 0))
    kern = functools.partial(_attn_kernel, hd=hd, n_q_tiles=n_q_tiles)
    return pl.pallas_call(
        kern,
        grid=(bsz, n_steps),
        in_specs=[q_spec(t) for t in range(n_q_tiles)]
        + [pl.BlockSpec((1, n_kv, n_keys, hd), lambda b, i: (b, 0, 0, 0)),
           pl.BlockSpec((1, n_kv, hd, n_keys), lambda b, i: (b, 0, 0, 0))],
        out_specs=pl.BlockSpec((1, n_q_tiles * TM, n_kv * Q_PER_KV * hd), lambda b, i: (b, i, 0)),
        out_shape=jax.ShapeDtypeStruct((bsz, n_steps * n_q_tiles * TM, n_kv * Q_PER_KV * hd), BF16),
        compiler_params=_cparams(2),
        name="gqa_attention",
    )(*([q_t] * n_q_tiles), k, v_t)


def _odd_out_kernel(*refs, pairs, n_attn, pool_group, seq_latent):
    per = 5 + n_attn
    pw_ref, ps_ref, w_ref, pg_ref, o_ref = refs[2 * per:]
    s = pl.program_id(0)
    tiles = []
    for sub in range(2):
        u_ref, up_ref, un_ref = refs[sub * per:sub * per + 3]
        att = refs[sub * per + 3:sub * per + 3 + n_attn]
        x_ref, gate_ref = refs[sub * per + 3 + n_attn:sub * per + 5 + n_attn]
        j = pairs.tile_in_sample(s, sub)
        dpl = u_ref.shape[1]
        attn = jnp.where(j == 0, att[1][0], att[0][...]) if n_attn == 2 else att[0][...]
        z_attn = jnp.dot(attn, w_ref[dpl:, :], preferred_element_type=F32)
        tiles.append((u_ref, up_ref, un_ref, x_ref, gate_ref, j, z_attn, dpl))
    for sub, (u_ref, up_ref, un_ref, x_ref, gate_ref, j, z_attn, dpl) in enumerate(tiles):
        pv, nv = _segment_halo_valid(j, pairs.tps_in, True)
        u = u_ref[...]
        uext = jnp.concatenate([up_ref[...] * pv, u, un_ref[...] * nv], axis=0)
        n_ext = uext.shape[0]
        s2 = uext[:n_ext - 1] + uext[1:]
        s4 = s2[:n_ext - 3] + s2[2:]
        s8 = s4[:n_ext - 7] + s4[4:]
        s16 = s8[:n_ext - 15] + s8[8:]
        sums = (s2, s4, s8, s16)
        seg_start = jnp.where(j == 0, 0, 1)
        seg_len = jnp.where(j == 0, TM, seq_latent).astype(F32)
        t = ((j - seg_start) * TM + lax.broadcasted_iota(jnp.int32, (TM, 1), 0)).astype(F32)
        lane_group = lax.broadcasted_iota(jnp.int32, (1, dpl), 1) // pool_group
        mean = jnp.zeros_like(u)
        for gi, w in enumerate(POOL_WINDOWS):
            hw = w // 2
            win = sums[gi][SMALL_HALO - hw:SMALL_HALO - hw + TM]
            cnt = jnp.minimum(t + hw, seg_len) - jnp.maximum(t - hw, 0.0)
            mean = jnp.where(lane_group == gi, win / cnt, mean)
        dpool = (mean - u).astype(BF16)
        y = jnp.dot(dpool, pw_ref[...], preferred_element_type=F32) * ps_ref[...]
        z = z_attn + jnp.dot(y.astype(BF16), w_ref[:dpl, :], preferred_element_type=F32)
        o_ref[sub * TM:(sub + 1) * TM, :] = x_ref[...] + gate_ref[0] * _rms(z, pg_ref[...])


def _odd_out(pool_u, attn, attn_ctx, xx, gate, post_g, pool_bd, pool_scale, w_out, *, n_batch, tile_off, seq_latent):
    bsz, tt, d = xx.shape
    tps = tt // TM
    tps_out = tps - tile_off
    assert (bsz * tps_out) % 2 == 0
    dp = pool_u.shape[-1]
    da = attn.shape[-1]
    pairs = _PairTiles(tps, tps_out, tile_off, n_batch)
    lat = _PairTiles(attn.shape[1] // TM, tps_out, tile_off, n_batch)
    flat = lambda a: a.reshape(a.shape[0] * a.shape[1], a.shape[2])
    uf, xf, af = flat(pool_u), flat(xx), flat(attn)
    in_specs, args = [], []
    for sub in range(2):
        def attn_index(s, sub=sub):
            b, j = lat.bj(s, sub)
            return (b * lat.tps_in + jnp.maximum(j - 1, 0), 0)
        in_specs += [pairs.tile(sub, dp), pairs.halo(sub, dp, SMALL_HALO, False), pairs.halo(sub, dp, SMALL_HALO, True),
                     pl.BlockSpec((TM, da), attn_index)]
        args += [uf, uf, uf, af]
        if tile_off == 0:
            in_specs.append(pl.BlockSpec((1, TM, da), lambda s, sub=sub: (pairs.bj(s, sub)[0], 0, 0)))
            args.append(attn_ctx)
        in_specs += [pairs.tile(sub, d), pairs.mod(sub, d)]
        args += [xf, gate]
    in_specs += [_const_spec(pool_bd.shape), _const_spec((1, dp)), _const_spec(w_out.shape), _const_spec((1, d))]
    args += [pool_bd, pool_scale, w_out, post_g]
    kern = functools.partial(_odd_out_kernel, pairs=pairs, n_attn=2 if tile_off == 0 else 1,
                             pool_group=dp // len(POOL_WINDOWS), seq_latent=seq_latent)
    out = pl.pallas_call(
        kern,
        grid=(bsz * tps_out // 2,),
        in_specs=in_specs,
        out_specs=pl.BlockSpec((2 * TM, d), lambda s: (s, 0)),
        out_shape=jax.ShapeDtypeStruct((bsz * tps_out * TM, d), F32),
        compiler_params=_cparams(1),
        name="odd_out",
    )(*args)
    return out.reshape(bsz, tps_out * TM, d)


def _ffn_kernel(x_ref, xp_ref, xn_ref, sha_ref, shb_ref, sca_ref, scb_ref, gta_ref, gtb_ref, g_ref, pg_ref,
                wi_ref, cw_ref, wo_ref, o_ref, *, tiles_per_sample, has_ctx):
    d_ff = wo_ref.shape[0]
    n_chunks = d_ff // FF_CHUNK
    s = pl.program_id(0)
    xs = (x_ref[:TM], x_ref[TM:])
    halos = ((xp_ref[...], xs[1][:SMALL_HALO]), (xs[0][TM - SMALL_HALO:], xn_ref[...]))
    mods = ((sha_ref, sca_ref, gta_ref), (shb_ref, scb_ref, gtb_ref))
    row = lax.broadcasted_iota(jnp.int32, (TM + 2 * SMALL_HALO, 1), 0)

    def normed(sub):
        j = lax.rem(2 * s + sub, tiles_per_sample)
        pv, nv = _segment_halo_valid(j, tiles_per_sample, has_ctx)
        xe = jnp.concatenate([halos[sub][0], xs[sub], halos[sub][1]], axis=0)
        he = _rms(xe, g_ref[...]) * (1.0 + mods[sub][1][0]) + mods[sub][0][0]
        keep = jnp.where(row < SMALL_HALO, pv, jnp.where(row >= SMALL_HALO + TM, nv, 1.0))
        return (he * keep).astype(BF16)

    def up_proj(he, c):
        lo = c * FF_CHUNK
        return (jnp.dot(he, wi_ref[:, lo:lo + FF_CHUNK], preferred_element_type=F32),
                jnp.dot(he[SMALL_HALO:SMALL_HALO + TM], wi_ref[:, d_ff + lo:d_ff + lo + FF_CHUNK],
                        preferred_element_type=F32))

    hes = [normed(0), None]
    items = [(sub, c) for sub in range(2) for c in range(n_chunks)]

    def issue(n):
        nsub, nc = items[n]
        if hes[nsub] is None:
            hes[nsub] = normed(nsub)
        return up_proj(hes[nsub], nc)

    ahead = [issue(n) for n in range(FF_LOOKAHEAD)]
    acc = None
    for n, (sub, c) in enumerate(items):
        g, v = ahead.pop(0)
        if n + FF_LOOKAHEAD < len(items):
            ahead.append(issue(n + FF_LOOKAHEAD))
        cw = cw_ref[:, c * FF_CHUNK:(c + 1) * FF_CHUNK]
        gc = (cw[0:1] * g[SMALL_HALO - 1:SMALL_HALO - 1 + TM] + cw[1:2] * g[SMALL_HALO:SMALL_HALO + TM]
              + cw[2:3] * g[SMALL_HALO + 1:SMALL_HALO + 1 + TM])
        u = (gc * jax.nn.sigmoid(gc) * v).astype(BF16)
        part = jnp.dot(u, wo_ref[c * FF_CHUNK:(c + 1) * FF_CHUNK, :], preferred_element_type=F32)
        acc = part if c == 0 else acc + part
        if c == n_chunks - 1:
            o_ref[sub * TM:(sub + 1) * TM, :] = xs[sub] + mods[sub][2][0] * _rms(acc, pg_ref[...])


def _ffn(xx, shift, scale, gate, pre_g, post_g, w_in, conv_w, w_out, *, n_batch, has_ctx):
    bsz, tt, d = xx.shape
    tps = tt // TM
    n_tiles = bsz * tps
    assert n_tiles % 2 == 0
    hb = TM // SMALL_HALO
    n_hblk = bsz * tt // SMALL_HALO

    def mod_spec(sub):
        def index(s):
            g = 2 * s + sub
            b = g // tps
            return (jnp.where(g % tps == 0, n_batch, b) if has_ctx else b, 0, 0)
        return pl.BlockSpec((1, 1, d), index)

    kern = functools.partial(_ffn_kernel, tiles_per_sample=tps, has_ctx=has_ctx)
    xf = xx.reshape(bsz * tt, d)
    out = pl.pallas_call(
        kern,
        grid=(n_tiles // 2,),
        in_specs=[pl.BlockSpec((2 * TM, d), lambda s: (s, 0)),
                  pl.BlockSpec((SMALL_HALO, d), lambda s: (jnp.maximum(2 * s * hb - 1, 0), 0)),
                  pl.BlockSpec((SMALL_HALO, d), lambda s: (jnp.minimum((2 * s + 2) * hb, n_hblk - 1), 0)),
                  mod_spec(0), mod_spec(1), mod_spec(0), mod_spec(1), mod_spec(0), mod_spec(1),
                  _const_spec((1, d)), _const_spec((1, d)),
                  _const_spec(w_in.shape), _const_spec(conv_w.shape), _const_spec(w_out.shape)],
        out_specs=pl.BlockSpec((2 * TM, d), lambda s: (s, 0)),
        out_shape=jax.ShapeDtypeStruct((bsz * tt, d), F32),
        compiler_params=_cparams(1),
        name="conv_ffn",
    )(xf, xf, xf, shift, shift, scale, scale, gate, gate, pre_g, post_g, w_in, conv_w, w_out)
    return out.reshape(bsz, tt, d)


def _rope_tables_t(seq, ctx_len, hd):
    rows = seq // GRID_W
    row = jnp.repeat(jnp.arange(rows), GRID_W).astype(F32)
    col = jnp.tile(jnp.arange(GRID_W), rows).astype(F32)
    n_freq = hd // 4
    inv = ROPE_THETA ** (-jnp.arange(n_freq, dtype=F32) / n_freq)
    ang = jnp.concatenate([row[:, None] * inv, col[:, None] * inv], axis=-1)
    cos = jnp.concatenate([jnp.ones((ctx_len, hd // 2), F32), jnp.cos(ang)], axis=0)
    sin = jnp.concatenate([jnp.zeros((ctx_len, hd // 2), F32), jnp.sin(ang)], axis=0)
    return cos.T, sin.T


def kernel(x, c, ctx, c_ctx, ada_w, ada_b, mix_pre_g, mix_post_g, ffn_pre_g, ffn_post_g, ffn_w_in, ffn_conv_w,
           ffn_w_out, even_w_in, even_w_out, conv_dw_w, conv_ln_g, conv_ln_b, mlstm_gate_b, mlstm_norm_g,
           odd_w_in, odd_w_out, pool_w, pool_scale, q_norm_g, k_norm_g):
    bsz, seq, d = x.shape
    ctx_len = ctx.shape[1]
    depth = ada_w.shape[0]
    assert ctx_len == TM and seq % TM == 0 and seq % GRID_W == 0 and depth >= 1
    dc = conv_dw_w.shape[2]
    dm = mlstm_norm_g.shape[1]
    n_heads = mlstm_gate_b.shape[2]
    hd_m = dm // n_heads
    assert n_heads % 2 == 0 and even_w_in.shape[2] == 2 * dc + 4 * dm + 4 * n_heads
    hd = q_norm_g.shape[1]
    dp = pool_scale.shape[1]
    n_kv = (odd_w_in.shape[2] - dp - (odd_w_out.shape[1] - dp)) // (2 * hd)
    n_q = (odd_w_out.shape[1] - dp) // hd
    assert n_q == n_kv * Q_PER_KV
    assert ffn_w_out.shape[1] % FF_CHUNK == 0

    pad = (-(bsz + 1)) % 8
    cc = jnp.concatenate([c, c_ctx[None, :], jnp.zeros((pad, d), F32)], axis=0)
    mods = _ada_all(cc, ada_w, ada_b)

    def mod(l, k):
        return mods[l, :bsz + 1, k * d:(k + 1) * d].reshape(bsz + 1, 1, d)

    cos_t, sin_t = _rope_tables_t(seq, ctx_len, hd)
    perm = np.concatenate([np.arange(0, hd, 2), np.arange(1, hd, 2)])

    xx = (ctx, x)
    row2 = lambda a: a.reshape(1, -1)

    for l in range(depth):
        last = l == depth - 1
        if l % 2 == 0:
            e = l // 2
            w_in = even_w_in[e]
            n_main = 2 * dc + 4 * dm
            w_main = w_in[:, :n_main].astype(BF16)
            wg = w_in[:, n_main:].reshape(d, 4, n_heads // 2, 2)
            w_gate_t = wg.transpose(2, 1, 3, 0).reshape(4 * n_heads, d).astype(BF16)
            gate_b_col = mlstm_gate_b[e].reshape(4, n_heads // 2, 2).transpose(1, 0, 2).reshape(4 * n_heads, 1)
            w_k_t = w_in[:, 2 * dc + dm:2 * dc + 2 * dm].T.astype(BF16)
            u, q, k_t, v, og, gt = _even_proj(xx, mod(l, 0), mod(l, 1), row2(mix_pre_g[l]), w_main, w_k_t, w_gate_t,
                                              gate_b_col, n_batch=bsz, dc=dc, dm=dm, hd=hd_m)
            mm = _mlstm(q, k_t, v, og, gt, row2(mlstm_norm_g[e]), hd=hd_m, n_ctx_win=ctx_len // MLSTM_WIN)
            xx = _even_out(u, mm, xx, mod(l, 2), row2(mix_post_g[l]), conv_dw_w[e], row2(conv_ln_g[e]),
                           row2(conv_ln_b[e]), even_w_out[e].astype(BF16), n_batch=bsz)
        else:
            o = l // 2
            w_in = odd_w_in[o]
            w_pool = w_in[:, :dp].astype(BF16)
            w_qk = w_in[:, dp:dp + (n_q + n_kv) * hd].reshape(d, n_q + n_kv, hd)[:, :, perm]
            w_qk_t = w_qk.reshape(d, -1).T.astype(BF16)
            w_v_t = w_in[:, dp + (n_q + n_kv) * hd:].T.astype(BF16)
            pool_u, q_t, k, v_t = _odd_proj(xx, mod(l, 0), mod(l, 1), row2(mix_pre_g[l]), w_pool, w_qk_t, w_v_t,
                                            cos_t, sin_t, q_norm_g[o][perm].reshape(hd, 1),
                                            k_norm_g[o][perm].reshape(hd, 1), n_batch=bsz, hd=hd, n_q=n_q, n_kv=n_kv)
            tile_off = 1 if last else 0
            attn = _attention(q_t, k, v_t, first_tile=1, n_q_tiles=2, n_keys=k.shape[2])
            attn_ctx = None if last else _attention(q_t, k, v_t, first_tile=0, n_q_tiles=1, n_keys=ctx_len)
            pg = dp // len(POOL_WINDOWS)
            pool_bd = jnp.zeros((dp, dp), F32)
            for gi in range(len(POOL_WINDOWS)):
                pool_bd = pool_bd.at[gi * pg:(gi + 1) * pg, gi * pg:(gi + 1) * pg].set(pool_w[o, gi])
            xx = _odd_out(pool_u, attn, attn_ctx, xx, mod(l, 2), row2(mix_post_g[l]), pool_bd.astype(BF16),
                          row2(pool_scale[o]), odd_w_out[o].astype(BF16), n_batch=bsz, tile_off=tile_off,
                          seq_latent=seq)
        xx = _ffn(xx, mod(l, 3), mod(l, 4), mod(l, 5), row2(ffn_pre_g[l]), row2(ffn_post_g[l]),
                  ffn_w_in[l].astype(BF16), ffn_conv_w[l], ffn_w_out[l].astype(BF16), n_batch=bsz,
                  has_ctx=xx.shape[1] != seq)
    return xx if xx.shape[1] == seq else xx[:, ctx_len:]
```

```python
import functools

import jax
import jax.numpy as jnp
import numpy as np
from jax import lax
from jax.experimental import pallas as pl
from jax.experimental.pallas import tpu as pltpu

F32 = jnp.float32
BF16 = jnp.bfloat16

GRID_W = 64
MLSTM_WIN = 128
POOL_WINDOWS = (2, 4, 8, 16)
Q_PER_KV = 3
ROPE_THETA = 10000.0
EPS = 1e-6
LOG2_E = 1.4426950408889634

TM = 256
CONV_HALO = 16
SMALL_HALO = 8
FF_CHUNK = 256
FF_LOOKAHEAD = 2
FF_TILES = 4
ATTN_KEY_CHUNK = 256
VMEM_LIMIT = 56 * 1024 * 1024


def _cparams(n_axes, vmem=VMEM_LIMIT):
    return pltpu.CompilerParams(dimension_semantics=("arbitrary",) * n_axes, vmem_limit_bytes=vmem)


def _const_spec(shape):
    nd = len(shape)
    return pl.BlockSpec(shape, lambda *_: (0,) * nd)


def _rms(x, g):
    return x * lax.rsqrt(jnp.mean(x * x, axis=-1, keepdims=True) + EPS) * g


def _segment_halo_valid(j, n_tiles_total, has_ctx):
    if has_ctx:
        prev_ok = jnp.logical_and(j != 0, j != 1)
        next_ok = jnp.logical_and(j != 0, j != n_tiles_total - 1)
    else:
        prev_ok = j != 0
        next_ok = j != n_tiles_total - 1
    return prev_ok.astype(F32), next_ok.astype(F32)


def _ada_kernel(c_ref, w_ref, b_ref, o_ref):
    c = c_ref[...]
    s = c * jax.nn.sigmoid(c)
    o_ref[0] = jnp.dot(s, w_ref[0], preferred_element_type=F32, precision=lax.Precision.HIGHEST) + b_ref[0]


def _ada_all(cc, ada_w, ada_b):
    depth, d, n = ada_w.shape
    nb = n // 4
    rows = cc.shape[0]
    return pl.pallas_call(
        _ada_kernel,
        grid=(depth, n // nb),
        in_specs=[pl.BlockSpec((rows, d), lambda l, j: (0, 0)),
                  pl.BlockSpec((1, d, nb), lambda l, j: (l, 0, j)),
                  pl.BlockSpec((1, 1, nb), lambda l, j: (l, 0, j))],
        out_specs=pl.BlockSpec((1, rows, nb), lambda l, j: (l, 0, j)),
        out_shape=jax.ShapeDtypeStruct((depth, rows, n), F32),
        compiler_params=_cparams(2),
        name="ada_mod",
    )(cc, ada_w, ada_b.reshape(depth, 1, n))


def _stream_specs(stream, d):
    if isinstance(stream, tuple):
        return [pl.BlockSpec((1, TM, d), lambda b, i: (b, 0, 0)),
                pl.BlockSpec((1, TM, d), lambda b, i: (b, jnp.maximum(i - 1, 0), 0))]
    return [pl.BlockSpec((1, TM, d), lambda b, i: (b, i, 0))]


def _stream_tile(refs, i):
    if len(refs) == 2:
        return jnp.where(i == 0, refs[0][0], refs[1][0])
    return refs[0][0]


class _PairTiles:
    def __init__(self, tps_in, tps_out, tile_off, n_batch):
        self.tps_in, self.tps_out, self.tile_off, self.n_batch = tps_in, tps_out, tile_off, n_batch

    def bj(self, s, sub):
        g = 2 * s + sub
        return g // self.tps_out, g % self.tps_out + self.tile_off

    def tile(self, sub, width):
        def index(s):
            b, j = self.bj(s, sub)
            return (b * self.tps_in + j, 0)
        return pl.BlockSpec((TM, width), index)

    def halo(self, sub, width, rows, after):
        per_tile = TM // rows
        last = self.n_batch * self.tps_in * per_tile - 1

        def index(s):
            b, j = self.bj(s, sub)
            t = (b * self.tps_in + j) * per_tile
            return (jnp.minimum(t + per_tile, last) if after else jnp.maximum(t - 1, 0), 0)
        return pl.BlockSpec((rows, width), index)

    def mod(self, sub, d):
        def index(s):
            b, j = self.bj(s, sub)
            return (jnp.where(j == 0, self.n_batch, b), 0, 0)
        return pl.BlockSpec((1, 1, d), index)

    def tile_in_sample(self, s, sub):
        return lax.rem(2 * s + sub, self.tps_out) + self.tile_off


def _mod_spec(d, off, n_batch):
    if off == 0:
        return pl.BlockSpec((1, 1, d), lambda b, i: (jnp.where(i == 0, n_batch, b), 0, 0))
    return pl.BlockSpec((1, 1, d), lambda b, i: (b, 0, 0))


def _even_proj_kernel(*refs, n_x, dc, dm, k_scale):
    (sh_ref, sc_ref, g_ref, w_ref, wk_ref, wg_ref, gb_ref,
     u_ref, q_ref, kt_ref, v_ref, o_ref, gt_ref) = refs[n_x:]
    x = _stream_tile(refs[:n_x], pl.program_id(1))
    h = (_rms(x, g_ref[...]) * (1.0 + sc_ref[0]) + sh_ref[0]).astype(BF16)
    nt_dims = (((1,), (1,)), ((), ()))

    def proj(lo, hi):
        return jnp.dot(h, w_ref[:, lo:hi], preferred_element_type=F32)

    off = 2 * dc
    p_glu = proj(0, off)
    p_q = proj(off, off + dm)
    u_ref[0] = (p_glu[:, :dc] * jax.nn.sigmoid(p_glu[:, dc:])).astype(BF16)
    k_t = lax.dot_general(wk_ref[...], h, nt_dims, preferred_element_type=F32)
    q_ref[0] = p_q.astype(BF16)
    p_v = proj(off + 2 * dm, off + 3 * dm)
    kt_ref[0] = (k_t * k_scale).astype(BF16)
    p_o = proj(off + 3 * dm, off + 4 * dm)
    v_ref[0] = p_v.astype(BF16)
    gt = lax.dot_general(wg_ref[...], h, nt_dims, preferred_element_type=F32) + gb_ref[...]
    o_ref[0] = jax.nn.sigmoid(p_o).astype(BF16)
    row = lax.broadcasted_iota(jnp.int32, gt.shape, 0)
    gt = jnp.where((row & 2) != 0, jax.nn.log_sigmoid(gt), gt)
    for pr in range(gt.shape[0] // 8):
        for c in range(TM // MLSTM_WIN):
            gt_ref[0, pr, c] = gt[pr * 8:(pr + 1) * 8, c * MLSTM_WIN:(c + 1) * MLSTM_WIN]


def _even_proj(stream, shift, scale, pre_g, w_main, w_k_t, w_gate_t, gate_b_col, *, n_batch, dc, dm, hd):
    xs = stream if isinstance(stream, tuple) else (stream,)
    bsz, d = xs[0].shape[0], xs[0].shape[2]
    tt = sum(a.shape[1] for a in xs)
    nt = tt // TM
    n_sub = TM // MLSTM_WIN
    n_pairs = w_gate_t.shape[0] // 8
    n_win = tt // MLSTM_WIN
    tok = lambda w: pl.BlockSpec((1, TM, w), lambda b, i: (b, i, 0))
    kern = functools.partial(_even_proj_kernel, n_x=len(xs), dc=dc, dm=dm, k_scale=float(hd) ** -0.5)
    return pl.pallas_call(
        kern,
        grid=(bsz, nt),
        in_specs=_stream_specs(stream, d) + [_mod_spec(d, 0, n_batch), _mod_spec(d, 0, n_batch), _const_spec((1, d)),
                  _const_spec(w_main.shape), _const_spec(w_k_t.shape), _const_spec(w_gate_t.shape),
                  _const_spec(gate_b_col.shape)],
        out_specs=[tok(dc), tok(dm), pl.BlockSpec((1, dm, TM), lambda b, i: (b, 0, i)), tok(dm), tok(dm),
                   pl.BlockSpec((1, n_pairs, n_sub, 8, MLSTM_WIN), lambda b, i: (b, 0, i, 0, 0))],
        out_shape=[jax.ShapeDtypeStruct((bsz, tt, dc), BF16), jax.ShapeDtypeStruct((bsz, tt, dm), BF16),
                   jax.ShapeDtypeStruct((bsz, dm, tt), BF16), jax.ShapeDtypeStruct((bsz, tt, dm), BF16),
                   jax.ShapeDtypeStruct((bsz, tt, dm), BF16),
                   jax.ShapeDtypeStruct((bsz, n_pairs, n_win, 8, MLSTM_WIN), F32)],
        compiler_params=_cparams(2),
        name="even_proj",
    )(*xs, shift, scale, pre_g, w_main, w_k_t, w_gate_t, gate_b_col)


def _mlstm_kernel(q_ref, kt_ref, v_ref, og_ref, gt_ref, ng_ref, out_ref,
                  hf_ref, hb_ref, c_ref, grow_ref, gmax_ref, blast_ref, mprev_ref,
                  *, hd, n_win, n_ctx_win, tt):
    W = MLSTM_WIN
    rowi = lax.broadcasted_iota(jnp.int32, (W, W), 0)
    coli = lax.broadcasted_iota(jnp.int32, (W, W), 1)
    tri_by_dir = (coli <= rowi, coli >= rowi)
    ones_blk = jnp.ones((W, hd), BF16)

    def win_of(d, t):
        if d == 0:
            return t
        return jnp.where(t < n_ctx_win, n_ctx_win - 1 - t, n_win - 1 - (t - n_ctx_win))

    for d in range(2):
        cum = jnp.where(rowi <= coli if d == 0 else rowi >= coli, 1.0, 0.0)
        for hh in range(2):
            ci = d * 2 + hh
            i_rows = gt_ref[0, 0, :, 4 * d + hh, :]
            lf_rows = gt_ref[0, 0, :, 4 * d + 2 + hh, :]
            b_rows = jnp.dot(lf_rows, cum, preferred_element_type=F32, precision=lax.Precision.HIGHEST)
            g_rows = i_rows - b_rows
            grow_ref[ci, :n_win, :] = g_rows
            gmax_ref[ci, :n_win, :] = jnp.broadcast_to(jnp.max(g_rows, axis=1, keepdims=True), (n_win, W))
            blast_ref[ci, :n_win, :] = jnp.broadcast_to(jnp.sum(lf_rows, axis=1, keepdims=True), (n_win, W))

    def scan(t, ms):
        out = []
        for d in range(2):
            w = win_of(d, t)
            for hh in range(2):
                ci = d * 2 + hh
                mprev_ref[ci, pl.ds(w, 1), :] = ms[ci]
                out.append(blast_ref[ci, pl.ds(w, 1), :] + jnp.maximum(ms[ci], gmax_ref[ci, pl.ds(w, 1), :]))
        return tuple(out)

    lax.fori_loop(0, n_win, scan, (jnp.zeros((1, W), F32),) * 4)

    c_ref[...] = jnp.zeros(c_ref.shape, F32)

    def step(t, carry):
        chains = []
        for d in range(2):
            w = win_of(d, t)
            r0 = pl.multiple_of(w * W, W)
            gt = gt_ref[0, 0, w]
            tri = tri_by_dir[d]
            for hh in range(2):
                ci = d * 2 + hh
                cols = slice(hh * hd, (hh + 1) * hd)
                lf_row = gt[4 * d + 2 + hh:4 * d + 3 + hh, :]
                g_row = grow_ref[ci, pl.ds(w, 1), :]
                m_prev = mprev_ref[ci, pl.ds(w, 1), :][:, :1]
                g_max = gmax_ref[ci, pl.ds(w, 1), :][:, :1]
                b_col = jnp.sum(jnp.where(tri, lf_row, 0.0), axis=1, keepdims=True)
                cg_col = jnp.max(jnp.where(tri, g_row, -jnp.inf), axis=1, keepdims=True)
                m_col = jnp.broadcast_to(jnp.maximum(m_prev, cg_col), (W, W))
                g_top = jnp.maximum(m_prev, g_max)
                kt_win = kt_ref[0, cols, pl.ds(r0, W)]
                chains.append(dict(
                    ci=ci, r0=r0, cols=cols, h_ref=hf_ref if d == 0 else hb_ref,
                    q_win=q_ref[0, pl.ds(r0, W), cols], kt_win=kt_win,
                    v_aug=jnp.concatenate([v_ref[0, pl.ds(r0, W), cols], ones_blk], axis=1),
                    decay=jnp.where(tri, jnp.exp(g_row - m_col), 0.0), w_col=jnp.exp(m_prev - m_col),
                    clamp=jnp.exp(-(b_col + m_col)), keep=jnp.exp(m_prev - g_top),
                    kte=(kt_win.astype(F32) * jnp.exp(g_row - g_top)).astype(BF16)))
        for ch in chains:
            ch["qk"] = jnp.dot(ch["q_win"], ch["kt_win"], preferred_element_type=F32)
        for ch in chains:
            ch["c_aug"] = c_ref[ch["ci"]]
            ch["qc"] = jnp.dot(ch["q_win"], ch["c_aug"].astype(BF16), preferred_element_type=F32)
        for ch in chains:
            ch["upd"] = jnp.dot(ch["kte"], ch["v_aug"], preferred_element_type=F32)
        for ch in chains:
            ch["sv"] = jnp.dot((ch["qk"] * ch["decay"]).astype(BF16), ch["v_aug"], preferred_element_type=F32)
        for ch in chains:
            wgt, qc, sv = ch["w_col"], ch["qc"], ch["sv"]
            num = wgt * qc[:, :hd] + sv[:, :hd]
            den = wgt * qc[:, hd:] + sv[:, hd:]
            ch["h_ref"][pl.ds(ch["r0"], W), ch["cols"]] = num / jnp.maximum(jnp.abs(den), ch["clamp"])
            c_ref[ch["ci"]] = ch["keep"] * ch["c_aug"] + ch["upd"]
        return carry

    assert hd == W
    lax.fori_loop(0, n_win, step, 0, unroll=2)

    def merge(t, carry):
        r0 = pl.multiple_of(t * TM, TM)
        hsum = hf_ref[pl.ds(r0, TM), :] + hb_ref[pl.ds(r0, TM), :]
        og = og_ref[0, pl.ds(r0, TM), :].astype(F32)
        ng = ng_ref[...]
        parts = []
        for hh in range(2):
            cols = slice(hh * hd, (hh + 1) * hd)
            parts.append(_rms(hsum[:, cols], ng[:, cols]))
        out_ref[0, pl.ds(r0, TM), :] = (jnp.concatenate(parts, axis=1) * og).astype(BF16)
        return carry

    lax.fori_loop(0, tt // TM, merge, 0, unroll=True)


def _mlstm(q, k_t, v, og, gt, norm_g, *, hd, n_ctx_win):
    bsz, tt, dm = q.shape
    n_pairs = gt.shape[1]
    n_win = gt.shape[2]
    pw = 2 * hd
    nw_pad = -(-n_win // 8) * 8
    tok = pl.BlockSpec((1, tt, pw), lambda b, p: (b, 0, p))
    kern = functools.partial(_mlstm_kernel, hd=hd, n_win=n_win, n_ctx_win=n_ctx_win, tt=tt)
    return pl.pallas_call(
        kern,
        grid=(bsz, n_pairs),
        in_specs=[tok, pl.BlockSpec((1, pw, tt), lambda b, p: (b, p, 0)), tok, tok,
                  pl.BlockSpec((1, 1, n_win, 8, MLSTM_WIN), lambda b, p: (b, p, 0, 0, 0)),
                  pl.BlockSpec((1, pw), lambda b, p: (0, p))],
        out_specs=tok,
        out_shape=jax.ShapeDtypeStruct((bsz, tt, dm), BF16),
        scratch_shapes=[pltpu.VMEM((tt, pw), F32), pltpu.VMEM((tt, pw), F32),
                        pltpu.VMEM((4, hd, 2 * hd), F32), pltpu.VMEM((4, nw_pad, MLSTM_WIN), F32),
                        pltpu.VMEM((4, nw_pad, MLSTM_WIN), F32), pltpu.VMEM((4, nw_pad, MLSTM_WIN), F32),
                        pltpu.VMEM((4, nw_pad, MLSTM_WIN), F32)],
        compiler_params=_cparams(2),
        name="mlstm",
    )(q, k_t, v, og, gt, norm_g)


def _even_out_kernel(*refs, pairs, n_res, width):
    per = 5 + n_res
    dw_ref, lg_ref, lb_ref, w_ref, pg_ref, o_ref, sh_ref = refs[2 * per:]
    s = pl.program_id(0)
    dw = dw_ref[...]
    base = CONV_HALO - width // 2
    span = sh_ref.shape[2]
    tiles = []
    for sub in range(2):
        u_ref, up_ref, un_ref, mm_ref = refs[sub * per:sub * per + 4]
        res = refs[sub * per + 4:sub * per + 4 + n_res]
        gate_ref = refs[sub * per + 4 + n_res]
        dcv = u_ref.shape[1]
        z_mix = jnp.dot(mm_ref[...], w_ref[dcv:, :], preferred_element_type=F32)
        tiles.append((u_ref, up_ref, un_ref, res, gate_ref, z_mix, dcv))
    for sub, (u_ref, up_ref, un_ref, res, gate_ref, z_mix, dcv) in enumerate(tiles):
        j = pairs.tile_in_sample(s, sub)
        pv, nv = _segment_halo_valid(j, pairs.tps_in, True)
        uext = jnp.concatenate([up_ref[...].astype(F32) * pv, u_ref[...].astype(F32), un_ref[...].astype(F32) * nv],
                               axis=0)
        for r in range(8):
            sh_ref[sub, r] = uext[base + r:base + r + span]
        acc = jnp.zeros((TM, dcv), F32)
        for t in range(width):
            acc = acc + dw[t:t + 1, :] * sh_ref[sub, t % 8, 8 * (t // 8):8 * (t // 8) + TM, :]
        mu = jnp.mean(acc, axis=-1, keepdims=True)
        cen = acc - mu
        var = jnp.mean(cen * cen, axis=-1, keepdims=True)
        y = cen * lax.rsqrt(var + EPS) * lg_ref[...] + lb_ref[...]
        conv_out = (y * jax.nn.sigmoid(y)).astype(BF16)
        z = z_mix + jnp.dot(conv_out, w_ref[:dcv, :], preferred_element_type=F32)
        x_res = jnp.where(j == 0, res[0][0], res[1][0]) if n_res == 2 else res[0][...]
        o_ref[sub * TM:(sub + 1) * TM, :] = x_res + gate_ref[0] * _rms(z, pg_ref[...])


def _even_out(u, mm, stream, gate, post_g, dw_w, ln_g, ln_b, w_out, *, n_batch):
    xs = stream if isinstance(stream, tuple) else (stream,)
    bsz, tt, dc = u.shape
    d = xs[0].shape[2]
    tps = tt // TM
    assert (bsz * tps) % 2 == 0
    pairs = _PairTiles(tps, tps, 0, n_batch)
    flat = lambda a: a.reshape(bsz * tt, a.shape[2])
    uf, mmf = flat(u), flat(mm)
    in_specs, args = [], []
    for sub in range(2):
        in_specs += [pairs.tile(sub, dc), pairs.halo(sub, dc, CONV_HALO, False), pairs.halo(sub, dc, CONV_HALO, True),
                     pairs.tile(sub, mm.shape[2])]
        args += [uf, uf, uf, mmf]
        if len(xs) == 2:
            def ctx_index(s, sub=sub):
                return (pairs.bj(s, sub)[0], 0, 0)

            def lat_index(s, sub=sub):
                b, j = pairs.bj(s, sub)
                return (b, jnp.maximum(j - 1, 0), 0)
            in_specs += [pl.BlockSpec((1, TM, d), ctx_index), pl.BlockSpec((1, TM, d), lat_index)]
            args += list(xs)
        else:
            in_specs.append(pairs.tile(sub, d))
            args.append(flat(xs[0]))
        in_specs.append(pairs.mod(sub, d))
        args.append(gate)
    in_specs += [_const_spec(dw_w.shape), _const_spec((1, dc)), _const_spec((1, dc)), _const_spec(w_out.shape),
                 _const_spec((1, d))]
    args += [dw_w, ln_g, ln_b, w_out, post_g]
    kern = functools.partial(_even_out_kernel, pairs=pairs, n_res=len(xs), width=dw_w.shape[0])
    out = pl.pallas_call(
        kern,
        grid=(bsz * tps // 2,),
        in_specs=in_specs,
        out_specs=pl.BlockSpec((2 * TM, d), lambda s: (s, 0)),
        out_shape=jax.ShapeDtypeStruct((bsz * tt, d), F32),
        scratch_shapes=[pltpu.VMEM((2, 8, TM + 8 * ((dw_w.shape[0] - 1) // 8), dc), F32)],
        compiler_params=_cparams(1),
        name="even_out",
    )(*args)
    return out.reshape(bsz, tt, d)


def _odd_proj_kernel(x_ref, sh_ref, sc_ref, g_ref, wp_ref, wqk_ref, wv_ref, cos_ref, sin_ref, qg_ref, kg_ref,
                     pool_ref, qt_ref, k_ref, vt_ref, *, hd, n_q, n_kv, q_scale):
    x = x_ref[0]
    h = (_rms(x, g_ref[...]) * (1.0 + sc_ref[0]) + sh_ref[0]).astype(BF16)
    nt_dims = (((1,), (1,)), ((), ()))
    n_heads = n_q + n_kv
    split = n_heads // 2
    pool = jnp.dot(h, wp_ref[...], preferred_element_type=F32)
    qk_parts = [lax.dot_general(wqk_ref[lo * hd:hi * hd, :], h, nt_dims, preferred_element_type=F32)
                for lo, hi in ((0, split), (split, n_heads))]
    v_t = lax.dot_general(wv_ref[...], h, nt_dims, preferred_element_type=F32)
    pool_ref[0] = pool
    cos = cos_ref[...]
    sin = sin_ref[...]
    half = hd // 2
    for hi in range(n_heads):
        part, base = (qk_parts[0], 0) if hi < split else (qk_parts[1], split)
        t = part[(hi - base) * hd:(hi - base + 1) * hd]
        gain = qg_ref[...] if hi < n_q else kg_ref[...]
        tn = t * lax.rsqrt(jnp.mean(t * t, axis=0, keepdims=True) + EPS) * gain
        x1 = tn[:half]
        x2 = tn[half:]
        rot = jnp.concatenate([x1 * cos - x2 * sin, x1 * sin + x2 * cos], axis=0)
        if hi < n_q:
            kvh, gq = divmod(hi, Q_PER_KV)
            qt_ref[0, kvh, 0, :, gq * TM:(gq + 1) * TM] = (rot * q_scale).astype(BF16)
        else:
            k_ref[0, hi - n_q] = rot.T.astype(BF16)
    for kvh in range(n_kv):
        vt_ref[0, kvh] = v_t[kvh * hd:(kvh + 1) * hd].astype(BF16)


def _odd_proj(xx, shift, scale, pre_g, w_pool, w_qk_t, w_v_t, cos_t, sin_t, qg_col, kg_col, *, n_batch, hd, n_q, n_kv):
    bsz, tt, d = xx.shape
    nt = tt // TM
    dp = w_pool.shape[1]
    kern = functools.partial(_odd_proj_kernel, hd=hd, n_q=n_q, n_kv=n_kv, q_scale=float(hd) ** -0.5 * LOG2_E)
    return pl.pallas_call(
        kern,
        grid=(bsz, nt),
        in_specs=[pl.BlockSpec((1, TM, d), lambda b, i: (b, i, 0)),
                  _mod_spec(d, 0, n_batch), _mod_spec(d, 0, n_batch), _const_spec((1, d)),
                  _const_spec(w_pool.shape), _const_spec(w_qk_t.shape), _const_spec(w_v_t.shape),
                  pl.BlockSpec((hd // 2, TM), lambda b, i: (0, i)), pl.BlockSpec((hd // 2, TM), lambda b, i: (0, i)),
                  _const_spec((hd, 1)), _const_spec((hd, 1))],
        out_specs=[pl.BlockSpec((1, TM, dp), lambda b, i: (b, i, 0)),
                   pl.BlockSpec((1, n_kv, 1, hd, Q_PER_KV * TM), lambda b, i: (b, 0, i, 0, 0)),
                   pl.BlockSpec((1, n_kv, TM, hd), lambda b, i: (b, 0, i, 0)),
                   pl.BlockSpec((1, n_kv, hd, TM), lambda b, i: (b, 0, 0, i))],
        out_shape=[jax.ShapeDtypeStruct((bsz, tt, dp), F32),
                   jax.ShapeDtypeStruct((bsz, n_kv, nt, hd, Q_PER_KV * TM), BF16),
                   jax.ShapeDtypeStruct((bsz, n_kv, tt, hd), BF16),
                   jax.ShapeDtypeStruct((bsz, n_kv, hd, tt), BF16)],
        compiler_params=_cparams(2),
        name="odd_proj",
    )(xx, shift, scale, pre_g, w_pool, w_qk_t, w_v_t, cos_t, sin_t, qg_col, kg_col)


def _attn_kernel(*refs, hd, n_q_tiles):
    qt_refs, (k_ref, vt_ref, o_ref) = refs[:n_q_tiles], refs[n_q_tiles:]
    n_kv, n_keys = k_ref.shape[1], k_ref.shape[2]
    heads = [(t, kvh, g) for t in range(n_q_tiles) for kvh in range(n_kv) for g in range(Q_PER_KV)]
    kc = min(ATTN_KEY_CHUNK, n_keys)
    n_kc = n_keys // kc

    def fold8(a, op):
        return op(a.reshape(kc // 8, 8, TM), axis=0)

    def scores(t, kvh, g):
        q_t = qt_refs[t][0, kvh, 0, :, g * TM:(g + 1) * TM]
        chunks, mx = [], None
        for j in range(n_kc):
            s = jnp.dot(k_ref[0, kvh, j * kc:(j + 1) * kc, :], q_t, preferred_element_type=F32)
            chunks.append(s)
            m8 = fold8(s, jnp.max)
            mx = m8 if mx is None else jnp.maximum(mx, m8)
        return chunks, jnp.max(mx, axis=0, keepdims=True)

    nxt = scores(*heads[0])
    for n, (t, kvh, g) in enumerate(heads):
        chunks, mx = nxt
        if n + 1 < len(heads):
            nxt = scores(*heads[n + 1])
        den8, probs = None, []
        for s in chunks:
            p = jnp.exp2(s - mx)
            d8 = fold8(p, jnp.sum)
            den8 = d8 if den8 is None else den8 + d8
            probs.append(p.astype(BF16))
        den = jnp.sum(den8, axis=0, keepdims=True)
        o_t = jnp.dot(vt_ref[0, kvh], jnp.concatenate(probs, axis=0), preferred_element_type=F32)
        col = (kvh * Q_PER_KV + g) * hd
        o_ref[0, t * TM:(t + 1) * TM, col:col + hd] = (o_t / den).T.astype(BF16)


def _attention(q_t, k, v_t, *, first_tile, n_q_tiles, n_keys):
    bsz, n_kv, nt, hd, qw = q_t.shape
    n_steps = (nt - first_tile) // n_q_tiles if n_keys == k.shape[2] else 1
    assert n_keys % min(ATTN_KEY_CHUNK, n_keys) == 0
    q_spec = lambda t: pl.BlockSpec((1, n_kv, 1, hd, qw), lambda b, i: (b, 0, first_tile + i * n_q_tiles + t, 0, 0))
    kern = functools.partial(_attn_kernel, hd=hd, n_q_tiles=n_q_tiles)
    return pl.pallas_call(
        kern,
        grid=(bsz, n_steps),
        in_specs=[q_spec(t) for t in range(n_q_tiles)]
        + [pl.BlockSpec((1, n_kv, n_keys, hd), lambda b, i: (b, 0, 0, 0)),
           pl.BlockSpec((1, n_kv, hd, n_keys), lambda b, i: (b, 0, 0, 0))],
        out_specs=pl.BlockSpec((1, n_q_tiles * TM, n_kv * Q_PER_KV * hd), lambda b, i: (b, i, 0)),
        out_shape=jax.ShapeDtypeStruct((bsz, n_steps * n_q_tiles * TM, n_kv * Q_PER_KV * hd), BF16),
        compiler_params=_cparams(2),
        name="gqa_attention",
    )(*([q_t] * n_q_tiles), k, v_t)


def _odd_out_kernel(*refs, pairs, n_attn, pool_group, seq_latent):
    per = 5 + n_attn
    pw_ref, ps_ref, w_ref, pg_ref, o_ref = refs[2 * per:]
    s = pl.program_id(0)
    tiles = []
    for sub in range(2):
        u_ref, up_ref, un_ref = refs[sub * per:sub * per + 3]
        att = refs[sub * per + 3:sub * per + 3 + n_attn]
        x_ref, gate_ref = refs[sub * per + 3 + n_attn:sub * per + 5 + n_attn]
        j = pairs.tile_in_sample(s, sub)
        dpl = u_ref.shape[1]
        attn = jnp.where(j == 0, att[1][0], att[0][...]) if n_attn == 2 else att[0][...]
        z_attn = jnp.dot(attn, w_ref[dpl:, :], preferred_element_type=F32)
        tiles.append((u_ref, up_ref, un_ref, x_ref, gate_ref, j, z_attn, dpl))
    for sub, (u_ref, up_ref, un_ref, x_ref, gate_ref, j, z_attn, dpl) in enumerate(tiles):
        pv, nv = _segment_halo_valid(j, pairs.tps_in, True)
        u = u_ref[...]
        uext = jnp.concatenate([up_ref[...] * pv, u, un_ref[...] * nv], axis=0)
        n_ext = uext.shape[0]
        s2 = uext[:n_ext - 1] + uext[1:]
        s4 = s2[:n_ext - 3] + s2[2:]
        s8 = s4[:n_ext - 7] + s4[4:]
        s16 = s8[:n_ext - 15] + s8[8:]
        sums = (s2, s4, s8, s16)
        seg_start = jnp.where(j == 0, 0, 1)
        seg_len = jnp.where(j == 0, TM, seq_latent).astype(F32)
        t = ((j - seg_start) * TM + lax.broadcasted_iota(jnp.int32, (TM, 1), 0)).astype(F32)
        lane_group = lax.broadcasted_iota(jnp.int32, (1, dpl), 1) // pool_group
        mean = jnp.zeros_like(u)
        for gi, w in enumerate(POOL_WINDOWS):
            hw = w // 2
            win = sums[gi][SMALL_HALO - hw:SMALL_HALO - hw + TM]
            cnt = jnp.minimum(t + hw, seg_len) - jnp.maximum(t - hw, 0.0)
            mean = jnp.where(lane_group == gi, win / cnt, mean)
        dpool = (mean - u).astype(BF16)
        y = jnp.dot(dpool, pw_ref[...], preferred_element_type=F32) * ps_ref[...]
        z = z_attn + jnp.dot(y.astype(BF16), w_ref[:dpl, :], preferred_element_type=F32)
        o_ref[sub * TM:(sub + 1) * TM, :] = x_ref[...] + gate_ref[0] * _rms(z, pg_ref[...])


def _odd_out(pool_u, attn, attn_ctx, xx, gate, post_g, pool_bd, pool_scale, w_out, *, n_batch, tile_off, seq_latent):
    bsz, tt, d = xx.shape
    tps = tt // TM
    tps_out = tps - tile_off
    assert (bsz * tps_out) % 2 == 0
    dp = pool_u.shape[-1]
    da = attn.shape[-1]
    pairs = _PairTiles(tps, tps_out, tile_off, n_batch)
    lat = _PairTiles(attn.shape[1] // TM, tps_out, tile_off, n_batch)
    flat = lambda a: a.reshape(a.shape[0] * a.shape[1], a.shape[2])
    uf, xf, af = flat(pool_u), flat(xx), flat(attn)
    in_specs, args = [], []
    for sub in range(2):
        def attn_index(s, sub=sub):
            b, j = lat.bj(s, sub)
            return (b * lat.tps_in + jnp.maximum(j - 1, 0), 0)
        in_specs += [pairs.tile(sub, dp), pairs.halo(sub, dp, SMALL_HALO, False), pairs.halo(sub, dp, SMALL_HALO, True),
                     pl.BlockSpec((TM, da), attn_index)]
        args += [uf, uf, uf, af]
        if tile_off == 0:
            in_specs.append(pl.BlockSpec((1, TM, da), lambda s, sub=sub: (pairs.bj(s, sub)[0], 0, 0)))
            args.append(attn_ctx)
        in_specs += [pairs.tile(sub, d), pairs.mod(sub, d)]
        args += [xf, gate]
    in_specs += [_const_spec(pool_bd.shape), _const_spec((1, dp)), _const_spec(w_out.shape), _const_spec((1, d))]
    args += [pool_bd, pool_scale, w_out, post_g]
    kern = functools.partial(_odd_out_kernel, pairs=pairs, n_attn=2 if tile_off == 0 else 1,
                             pool_group=dp // len(POOL_WINDOWS), seq_latent=seq_latent)
    out = pl.pallas_call(
        kern,
        grid=(bsz * tps_out // 2,),
        in_specs=in_specs,
        out_specs=pl.BlockSpec((2 * TM, d), lambda s: (s, 0)),
        out_shape=jax.ShapeDtypeStruct((bsz * tps_out * TM, d), F32),
        compiler_params=_cparams(1),
        name="odd_out",
    )(*args)
    return out.reshape(bsz, tps_out * TM, d)


def _ffn_kernel(*refs, tiles_per_sample, has_ctx):
    x_ref, xp_ref, xn_ref = refs[:3]
    mod_refs = refs[3:3 + 3 * FF_TILES]
    g_ref, pg_ref, wi_ref, cw_ref, wo_ref, o_ref = refs[3 + 3 * FF_TILES:]
    d_ff = wo_ref.shape[0]
    n_chunks = d_ff // FF_CHUNK
    s = pl.program_id(0)
    xs = [x_ref[sub * TM:(sub + 1) * TM] for sub in range(FF_TILES)]
    row = lax.broadcasted_iota(jnp.int32, (TM + 2 * SMALL_HALO, 1), 0)

    def normed(sub):
        j = lax.rem(FF_TILES * s + sub, tiles_per_sample)
        pv, nv = _segment_halo_valid(j, tiles_per_sample, has_ctx)
        before = xp_ref[...] if sub == 0 else xs[sub - 1][TM - SMALL_HALO:]
        after = xn_ref[...] if sub == FF_TILES - 1 else xs[sub + 1][:SMALL_HALO]
        xe = jnp.concatenate([before, xs[sub], after], axis=0)
        he = _rms(xe, g_ref[...]) * (1.0 + mod_refs[3 * sub + 1][0]) + mod_refs[3 * sub][0]
        keep = jnp.where(row < SMALL_HALO, pv, jnp.where(row >= SMALL_HALO + TM, nv, 1.0))
        return (he * keep).astype(BF16)

    def up_proj(he, c):
        lo = c * FF_CHUNK
        return (jnp.dot(he, wi_ref[:, lo:lo + FF_CHUNK], preferred_element_type=F32),
                jnp.dot(he[SMALL_HALO:SMALL_HALO + TM], wi_ref[:, d_ff + lo:d_ff + lo + FF_CHUNK],
                        preferred_element_type=F32))

    hes = [None] * FF_TILES
    items = [(sub, c) for sub in range(FF_TILES) for c in range(n_chunks)]

    def issue(n):
        nsub, nc = items[n]
        if hes[nsub] is None:
            hes[nsub] = normed(nsub)
        return up_proj(hes[nsub], nc)

    ahead = [issue(n) for n in range(FF_LOOKAHEAD)]
    acc = None
    for n, (sub, c) in enumerate(items):
        g, v = ahead.pop(0)
        if n + FF_LOOKAHEAD < len(items):
            ahead.append(issue(n + FF_LOOKAHEAD))
        cw = cw_ref[:, c * FF_CHUNK:(c + 1) * FF_CHUNK]
        gc = (cw[0:1] * g[SMALL_HALO - 1:SMALL_HALO - 1 + TM] + cw[1:2] * g[SMALL_HALO:SMALL_HALO + TM]
              + cw[2:3] * g[SMALL_HALO + 1:SMALL_HALO + 1 + TM])
        u = (gc * jax.nn.sigmoid(gc) * v).astype(BF16)
        part = jnp.dot(u, wo_ref[c * FF_CHUNK:(c + 1) * FF_CHUNK, :], preferred_element_type=F32)
        acc = part if c == 0 else acc + part
        if c == n_chunks - 1:
            o_ref[sub * TM:(sub + 1) * TM, :] = xs[sub] + mod_refs[3 * sub + 2][0] * _rms(acc, pg_ref[...])


def _ffn(xx, shift, scale, gate, pre_g, post_g, w_in, conv_w, w_out, *, n_batch, has_ctx):
    bsz, tt, d = xx.shape
    tps = tt // TM
    n_tiles = bsz * tps
    assert n_tiles % FF_TILES == 0
    hb = TM // SMALL_HALO
    n_hblk = bsz * tt // SMALL_HALO

    def mod_spec(sub):
        def index(s):
            g = FF_TILES * s + sub
            b = g // tps
            return (jnp.where(g % tps == 0, n_batch, b) if has_ctx else b, 0, 0)
        return pl.BlockSpec((1, 1, d), index)

    kern = functools.partial(_ffn_kernel, tiles_per_sample=tps, has_ctx=has_ctx)
    xf = xx.reshape(bsz * tt, d)
    mod_specs, mod_args = [], []
    for sub in range(FF_TILES):
        mod_specs += [mod_spec(sub)] * 3
        mod_args += [shift, scale, gate]
    out = pl.pallas_call(
        kern,
        grid=(n_tiles // FF_TILES,),
        in_specs=[pl.BlockSpec((FF_TILES * TM, d), lambda s: (s, 0)),
                  pl.BlockSpec((SMALL_HALO, d), lambda s: (jnp.maximum(FF_TILES * s * hb - 1, 0), 0)),
                  pl.BlockSpec((SMALL_HALO, d), lambda s: (jnp.minimum(FF_TILES * (s + 1) * hb, n_hblk - 1), 0))]
        + mod_specs
        + [_const_spec((1, d)), _const_spec((1, d)),
           _const_spec(w_in.shape), _const_spec(conv_w.shape), _const_spec(w_out.shape)],
        out_specs=pl.BlockSpec((FF_TILES * TM, d), lambda s: (s, 0)),
        out_shape=jax.ShapeDtypeStruct((bsz * tt, d), F32),
        compiler_params=_cparams(1),
        name="conv_ffn",
    )(xf, xf, xf, *mod_args, pre_g, post_g, w_in, conv_w, w_out)
    return out.reshape(bsz, tt, d)


def _rope_tables_t(seq, ctx_len, hd):
    rows = seq // GRID_W
    row = jnp.repeat(jnp.arange(rows), GRID_W).astype(F32)
    col = jnp.tile(jnp.arange(GRID_W), rows).astype(F32)
    n_freq = hd // 4
    inv = ROPE_THETA ** (-jnp.arange(n_freq, dtype=F32) / n_freq)
    ang = jnp.concatenate([row[:, None] * inv, col[:, None] * inv], axis=-1)
    cos = jnp.concatenate([jnp.ones((ctx_len, hd // 2), F32), jnp.cos(ang)], axis=0)
    sin = jnp.concatenate([jnp.zeros((ctx_len, hd // 2), F32), jnp.sin(ang)], axis=0)
    return cos.T, sin.T


def kernel(x, c, ctx, c_ctx, ada_w, ada_b, mix_pre_g, mix_post_g, ffn_pre_g, ffn_post_g, ffn_w_in, ffn_conv_w,
           ffn_w_out, even_w_in, even_w_out, conv_dw_w, conv_ln_g, conv_ln_b, mlstm_gate_b, mlstm_norm_g,
           odd_w_in, odd_w_out, pool_w, pool_scale, q_norm_g, k_norm_g):
    bsz, seq, d = x.shape
    ctx_len = ctx.shape[1]
    depth = ada_w.shape[0]
    assert ctx_len == TM and seq % TM == 0 and seq % GRID_W == 0 and depth >= 1
    dc = conv_dw_w.shape[2]
    dm = mlstm_norm_g.shape[1]
    n_heads = mlstm_gate_b.shape[2]
    hd_m = dm // n_heads
    assert n_heads % 2 == 0 and even_w_in.shape[2] == 2 * dc + 4 * dm + 4 * n_heads
    hd = q_norm_g.shape[1]
    dp = pool_scale.shape[1]
    n_kv = (odd_w_in.shape[2] - dp - (odd_w_out.shape[1] - dp)) // (2 * hd)
    n_q = (odd_w_out.shape[1] - dp) // hd
    assert n_q == n_kv * Q_PER_KV
    assert ffn_w_out.shape[1] % FF_CHUNK == 0

    pad = (-(bsz + 1)) % 8
    cc = jnp.concatenate([c, c_ctx[None, :], jnp.zeros((pad, d), F32)], axis=0)
    mods = _ada_all(cc, ada_w, ada_b)

    def mod(l, k):
        return mods[l, :bsz + 1, k * d:(k + 1) * d].reshape(bsz + 1, 1, d)

    cos_t, sin_t = _rope_tables_t(seq, ctx_len, hd)
    perm = np.concatenate([np.arange(0, hd, 2), np.arange(1, hd, 2)])

    xx = (ctx, x)
    row2 = lambda a: a.reshape(1, -1)

    for l in range(depth):
        last = l == depth - 1
        if l % 2 == 0:
            e = l // 2
            w_in = even_w_in[e]
            n_main = 2 * dc + 4 * dm
            w_main = w_in[:, :n_main].astype(BF16)
            wg = w_in[:, n_main:].reshape(d, 4, n_heads // 2, 2)
            w_gate_t = wg.transpose(2, 1, 3, 0).reshape(4 * n_heads, d).astype(BF16)
            gate_b_col = mlstm_gate_b[e].reshape(4, n_heads // 2, 2).transpose(1, 0, 2).reshape(4 * n_heads, 1)
            w_k_t = w_in[:, 2 * dc + dm:2 * dc + 2 * dm].T.astype(BF16)
            u, q, k_t, v, og, gt = _even_proj(xx, mod(l, 0), mod(l, 1), row2(mix_pre_g[l]), w_main, w_k_t, w_gate_t,
                                              gate_b_col, n_batch=bsz, dc=dc, dm=dm, hd=hd_m)
            mm = _mlstm(q, k_t, v, og, gt, row2(mlstm_norm_g[e]), hd=hd_m, n_ctx_win=ctx_len // MLSTM_WIN)
            xx = _even_out(u, mm, xx, mod(l, 2), row2(mix_post_g[l]), conv_dw_w[e], row2(conv_ln_g[e]),
                           row2(conv_ln_b[e]), even_w_out[e].astype(BF16), n_batch=bsz)
        else:
            o = l // 2
            w_in = odd_w_in[o]
            w_pool = w_in[:, :dp].astype(BF16)
            w_qk = w_in[:, dp:dp + (n_q + n_kv) * hd].reshape(d, n_q + n_kv, hd)[:, :, perm]
            w_qk_t = w_qk.reshape(d, -1).T.astype(BF16)
            w_v_t = w_in[:, dp + (n_q + n_kv) * hd:].T.astype(BF16)
            pool_u, q_t, k, v_t = _odd_proj(xx, mod(l, 0), mod(l, 1), row2(mix_pre_g[l]), w_pool, w_qk_t, w_v_t,
                                            cos_t, sin_t, q_norm_g[o][perm].reshape(hd, 1),
                                            k_norm_g[o][perm].reshape(hd, 1), n_batch=bsz, hd=hd, n_q=n_q, n_kv=n_kv)
            tile_off = 1 if last else 0
            attn = _attention(q_t, k, v_t, first_tile=1, n_q_tiles=2, n_keys=k.shape[2])
            attn_ctx = None if last else _attention(q_t, k, v_t, first_tile=0, n_q_tiles=1, n_keys=ctx_len)
            pg = dp // len(POOL_WINDOWS)
            pool_bd = jnp.zeros((dp, dp), F32)
            for gi in range(len(POOL_WINDOWS)):
                pool_bd = pool_bd.at[gi * pg:(gi + 1) * pg, gi * pg:(gi + 1) * pg].set(pool_w[o, gi])
            xx = _odd_out(pool_u, attn, attn_ctx, xx, mod(l, 2), row2(mix_post_g[l]), pool_bd.astype(BF16),
                          row2(pool_scale[o]), odd_w_out[o].astype(BF16), n_batch=bsz, tile_off=tile_off,
                          seq_latent=seq)
        xx = _ffn(xx, mod(l, 3), mod(l, 4), mod(l, 5), row2(ffn_pre_g[l]), row2(ffn_post_g[l]),
                  ffn_w_in[l].astype(BF16), ffn_conv_w[l], ffn_w_out[l].astype(BF16), n_batch=bsz,
                  has_ctx=xx.shape[1] != seq)
    return xx if xx.shape[1] == seq else xx[:, ctx_len:]
```

```python
import functools

import jax
import jax.numpy as jnp
import numpy as np
from jax import lax
from jax.experimental import pallas as pl
from jax.experimental.pallas import tpu as pltpu

F32 = jnp.float32
BF16 = jnp.bfloat16

GRID_W = 64
MLSTM_WIN = 128
POOL_WINDOWS = (2, 4, 8, 16)
Q_PER_KV = 3
ROPE_THETA = 10000.0
EPS = 1e-6
LOG2_E = 1.4426950408889634

TM = 256
CONV_HALO = 16
SMALL_HALO = 8
FF_CHUNK = 256
FF_LOOKAHEAD = 2
FF_TILES = 4
ATTN_KEY_CHUNK = 256
SUBLANES = 8
GATES_PER_PAIR = 8
VMEM_LIMIT = 56 * 1024 * 1024


def _cparams(n_axes, vmem=VMEM_LIMIT):
    return pltpu.CompilerParams(dimension_semantics=("arbitrary",) * n_axes, vmem_limit_bytes=vmem)


def _const_spec(shape):
    nd = len(shape)
    return pl.BlockSpec(shape, lambda *_: (0,) * nd)


def _rms(x, g):
    return x * lax.rsqrt(jnp.mean(x * x, axis=-1, keepdims=True) + EPS) * g


def _segment_halo_valid(j, n_tiles_total, has_ctx):
    if has_ctx:
        prev_ok = jnp.logical_and(j != 0, j != 1)
        next_ok = jnp.logical_and(j != 0, j != n_tiles_total - 1)
    else:
        prev_ok = j != 0
        next_ok = j != n_tiles_total - 1
    return prev_ok.astype(F32), next_ok.astype(F32)


def _ada_kernel(c_ref, w_ref, b_ref, o_ref):
    c = c_ref[...]
    s = c * jax.nn.sigmoid(c)
    o_ref[0] = jnp.dot(s, w_ref[0], preferred_element_type=F32, precision=lax.Precision.HIGHEST) + b_ref[0]


def _ada_all(cc, ada_w, ada_b):
    depth, d, n = ada_w.shape
    nb = n // 4
    rows = cc.shape[0]
    return pl.pallas_call(
        _ada_kernel,
        grid=(depth, n // nb),
        in_specs=[pl.BlockSpec((rows, d), lambda l, j: (0, 0)),
                  pl.BlockSpec((1, d, nb), lambda l, j: (l, 0, j)),
                  pl.BlockSpec((1, 1, nb), lambda l, j: (l, 0, j))],
        out_specs=pl.BlockSpec((1, rows, nb), lambda l, j: (l, 0, j)),
        out_shape=jax.ShapeDtypeStruct((depth, rows, n), F32),
        compiler_params=_cparams(2),
        name="ada_mod",
    )(cc, ada_w, ada_b.reshape(depth, 1, n))


def _stream_specs(stream, d):
    if isinstance(stream, tuple):
        return [pl.BlockSpec((1, TM, d), lambda b, i: (b, 0, 0)),
                pl.BlockSpec((1, TM, d), lambda b, i: (b, jnp.maximum(i - 1, 0), 0))]
    return [pl.BlockSpec((1, TM, d), lambda b, i: (b, i, 0))]


def _stream_tile(refs, i):
    if len(refs) == 2:
        return jnp.where(i == 0, refs[0][0], refs[1][0])
    return refs[0][0]


class _PairTiles:
    def __init__(self, tps_in, tps_out, tile_off, n_batch):
        self.tps_in, self.tps_out, self.tile_off, self.n_batch = tps_in, tps_out, tile_off, n_batch

    def bj(self, s, sub):
        g = 2 * s + sub
        return g // self.tps_out, g % self.tps_out + self.tile_off

    def tile(self, sub, width):
        def index(s):
            b, j = self.bj(s, sub)
            return (b * self.tps_in + j, 0)
        return pl.BlockSpec((TM, width), index)

    def halo(self, sub, width, rows, after):
        per_tile = TM // rows
        last = self.n_batch * self.tps_in * per_tile - 1

        def index(s):
            b, j = self.bj(s, sub)
            t = (b * self.tps_in + j) * per_tile
            return (jnp.minimum(t + per_tile, last) if after else jnp.maximum(t - 1, 0), 0)
        return pl.BlockSpec((rows, width), index)

    def mod(self, sub, d):
        def index(s):
            b, j = self.bj(s, sub)
            return (jnp.where(j == 0, self.n_batch, b), 0, 0)
        return pl.BlockSpec((1, 1, d), index)

    def tile_in_sample(self, s, sub):
        return lax.rem(2 * s + sub, self.tps_out) + self.tile_off


def _mod_spec(d, off, n_batch):
    if off == 0:
        return pl.BlockSpec((1, 1, d), lambda b, i: (jnp.where(i == 0, n_batch, b), 0, 0))
    return pl.BlockSpec((1, 1, d), lambda b, i: (b, 0, 0))


def _even_proj_kernel(*refs, n_x, dc, dm, k_scale):
    (sh_ref, sc_ref, g_ref, w_ref, wk_ref, wg_ref, gb_ref,
     u_ref, q_ref, kt_ref, v_ref, o_ref, gt_ref) = refs[n_x:]
    x = _stream_tile(refs[:n_x], pl.program_id(1))
    h = (_rms(x, g_ref[...]) * (1.0 + sc_ref[0]) + sh_ref[0]).astype(BF16)
    nt_dims = (((1,), (1,)), ((), ()))

    def proj(lo, hi):
        return jnp.dot(h, w_ref[:, lo:hi], preferred_element_type=F32)

    off = 2 * dc
    p_glu = proj(0, off)
    p_q = proj(off, off + dm)
    u_ref[0] = (p_glu[:, :dc] * jax.nn.sigmoid(p_glu[:, dc:])).astype(BF16)
    k_t = lax.dot_general(wk_ref[...], h, nt_dims, preferred_element_type=F32)
    q_ref[0] = p_q.astype(BF16)
    p_v = proj(off + 2 * dm, off + 3 * dm)
    kt_ref[0] = (k_t * k_scale).astype(BF16)
    p_o = proj(off + 3 * dm, off + 4 * dm)
    v_ref[0] = p_v.astype(BF16)
    gt = lax.dot_general(wg_ref[...], h, nt_dims, preferred_element_type=F32) + gb_ref[...]
    o_ref[0] = jax.nn.sigmoid(p_o).astype(BF16)
    row = lax.broadcasted_iota(jnp.int32, gt.shape, 0)
    gt = jnp.where((row & 2) != 0, jax.nn.log_sigmoid(gt), gt)
    for pr in range(gt.shape[0] // GATES_PER_PAIR):
        for c in range(TM // MLSTM_WIN):
            gt_ref[0, pr, c] = gt[pr * GATES_PER_PAIR:(pr + 1) * GATES_PER_PAIR, c * MLSTM_WIN:(c + 1) * MLSTM_WIN]


def _even_proj(stream, shift, scale, pre_g, w_main, w_k_t, w_gate_t, gate_b_col, *, n_batch, dc, dm, hd):
    xs = stream if isinstance(stream, tuple) else (stream,)
    bsz, d = xs[0].shape[0], xs[0].shape[2]
    tt = sum(a.shape[1] for a in xs)
    nt = tt // TM
    n_sub = TM // MLSTM_WIN
    n_pairs = w_gate_t.shape[0] // GATES_PER_PAIR
    n_win = tt // MLSTM_WIN
    tok = lambda w: pl.BlockSpec((1, TM, w), lambda b, i: (b, i, 0))
    kern = functools.partial(_even_proj_kernel, n_x=len(xs), dc=dc, dm=dm, k_scale=float(hd) ** -0.5)
    return pl.pallas_call(
        kern,
        grid=(bsz, nt),
        in_specs=_stream_specs(stream, d) + [_mod_spec(d, 0, n_batch), _mod_spec(d, 0, n_batch), _const_spec((1, d)),
                  _const_spec(w_main.shape), _const_spec(w_k_t.shape), _const_spec(w_gate_t.shape),
                  _const_spec(gate_b_col.shape)],
        out_specs=[tok(dc), tok(dm), pl.BlockSpec((1, dm, TM), lambda b, i: (b, 0, i)), tok(dm), tok(dm),
                   pl.BlockSpec((1, n_pairs, n_sub, GATES_PER_PAIR, MLSTM_WIN), lambda b, i: (b, 0, i, 0, 0))],
        out_shape=[jax.ShapeDtypeStruct((bsz, tt, dc), BF16), jax.ShapeDtypeStruct((bsz, tt, dm), BF16),
                   jax.ShapeDtypeStruct((bsz, dm, tt), BF16), jax.ShapeDtypeStruct((bsz, tt, dm), BF16),
                   jax.ShapeDtypeStruct((bsz, tt, dm), BF16),
                   jax.ShapeDtypeStruct((bsz, n_pairs, n_win, GATES_PER_PAIR, MLSTM_WIN), F32)],
        compiler_params=_cparams(2),
        name="even_proj",
    )(*xs, shift, scale, pre_g, w_main, w_k_t, w_gate_t, gate_b_col)


def _mlstm_kernel(q_ref, kt_ref, v_ref, og_ref, gt_ref, ng_ref, out_ref,
                  hf_ref, hb_ref, c_ref, grow_ref, gmax_ref, blast_ref, mprev_ref,
                  *, hd, n_win, n_ctx_win, tt):
    W = MLSTM_WIN
    rowi = lax.broadcasted_iota(jnp.int32, (W, W), 0)
    coli = lax.broadcasted_iota(jnp.int32, (W, W), 1)
    tri_by_dir = (coli <= rowi, coli >= rowi)
    ones_blk = jnp.ones((W, hd), BF16)

    def win_of(d, t):
        if d == 0:
            return t
        return jnp.where(t < n_ctx_win, n_ctx_win - 1 - t, n_win - 1 - (t - n_ctx_win))

    for d in range(2):
        cum = jnp.where(rowi <= coli if d == 0 else rowi >= coli, 1.0, 0.0)
        for hh in range(2):
            ci = d * 2 + hh
            i_rows = gt_ref[0, 0, :, 4 * d + hh, :]
            lf_rows = gt_ref[0, 0, :, 4 * d + 2 + hh, :]
            b_rows = jnp.dot(lf_rows, cum, preferred_element_type=F32, precision=lax.Precision.HIGHEST)
            g_rows = i_rows - b_rows
            grow_ref[ci, :n_win, :] = g_rows
            gmax_ref[ci, :n_win, :] = jnp.broadcast_to(jnp.max(g_rows, axis=1, keepdims=True), (n_win, W))
            blast_ref[ci, :n_win, :] = jnp.broadcast_to(jnp.sum(lf_rows, axis=1, keepdims=True), (n_win, W))

    def scan(t, ms):
        out = []
        for d in range(2):
            w = win_of(d, t)
            for hh in range(2):
                ci = d * 2 + hh
                mprev_ref[ci, pl.ds(w, 1), :] = ms[ci]
                out.append(blast_ref[ci, pl.ds(w, 1), :] + jnp.maximum(ms[ci], gmax_ref[ci, pl.ds(w, 1), :]))
        return tuple(out)

    lax.fori_loop(0, n_win, scan, (jnp.zeros((1, W), F32),) * 4)

    c_ref[...] = jnp.zeros(c_ref.shape, F32)

    def step(t, carry):
        chains = []
        for d in range(2):
            w = win_of(d, t)
            r0 = pl.multiple_of(w * W, W)
            gt = gt_ref[0, 0, w]
            tri = tri_by_dir[d]
            for hh in range(2):
                ci = d * 2 + hh
                cols = slice(hh * hd, (hh + 1) * hd)
                lf_row = gt[4 * d + 2 + hh:4 * d + 3 + hh, :]
                g_row = grow_ref[ci, pl.ds(w, 1), :]
                m_prev = mprev_ref[ci, pl.ds(w, 1), :][:, :1]
                g_max = gmax_ref[ci, pl.ds(w, 1), :][:, :1]
                b_col = jnp.sum(jnp.where(tri, lf_row, 0.0), axis=1, keepdims=True)
                cg_col = jnp.max(jnp.where(tri, g_row, -jnp.inf), axis=1, keepdims=True)
                m_col = jnp.broadcast_to(jnp.maximum(m_prev, cg_col), (W, W))
                g_top = jnp.maximum(m_prev, g_max)
                kt_win = kt_ref[0, cols, pl.ds(r0, W)]
                chains.append(dict(
                    ci=ci, r0=r0, cols=cols, h_ref=hf_ref if d == 0 else hb_ref,
                    q_win=q_ref[0, pl.ds(r0, W), cols], kt_win=kt_win,
                    v_aug=jnp.concatenate([v_ref[0, pl.ds(r0, W), cols], ones_blk], axis=1),
                    decay=jnp.where(tri, jnp.exp(g_row - m_col), 0.0), w_col=jnp.exp(m_prev - m_col),
                    clamp=jnp.exp(-(b_col + m_col)), keep=jnp.exp(m_prev - g_top),
                    kte=(kt_win.astype(F32) * jnp.exp(g_row - g_top)).astype(BF16)))
        for ch in chains:
            ch["qk"] = jnp.dot(ch["q_win"], ch["kt_win"], preferred_element_type=F32)
        for ch in chains:
            ch["c_aug"] = c_ref[ch["ci"]]
            ch["qc"] = jnp.dot(ch["q_win"], ch["c_aug"].astype(BF16), preferred_element_type=F32)
        for ch in chains:
            ch["upd"] = jnp.dot(ch["kte"], ch["v_aug"], preferred_element_type=F32)
        for ch in chains:
            ch["sv"] = jnp.dot((ch["qk"] * ch["decay"]).astype(BF16), ch["v_aug"], preferred_element_type=F32)
        for ch in chains:
            wgt, qc, sv = ch["w_col"], ch["qc"], ch["sv"]
            num = wgt * qc[:, :hd] + sv[:, :hd]
            den = wgt * qc[:, hd:] + sv[:, hd:]
            ch["h_ref"][pl.ds(ch["r0"], W), ch["cols"]] = num / jnp.maximum(jnp.abs(den), ch["clamp"])
            c_ref[ch["ci"]] = ch["keep"] * ch["c_aug"] + ch["upd"]
        return carry

    assert hd == W
    lax.fori_loop(0, n_win, step, 0, unroll=2)

    def merge(t, carry):
        r0 = pl.multiple_of(t * TM, TM)
        hsum = hf_ref[pl.ds(r0, TM), :] + hb_ref[pl.ds(r0, TM), :]
        og = og_ref[0, pl.ds(r0, TM), :].astype(F32)
        ng = ng_ref[...]
        parts = []
        for hh in range(2):
            cols = slice(hh * hd, (hh + 1) * hd)
            parts.append(_rms(hsum[:, cols], ng[:, cols]))
        out_ref[0, pl.ds(r0, TM), :] = (jnp.concatenate(parts, axis=1) * og).astype(BF16)
        return carry

    lax.fori_loop(0, tt // TM, merge, 0, unroll=True)


def _mlstm(q, k_t, v, og, gt, norm_g, *, hd, n_ctx_win):
    bsz, tt, dm = q.shape
    n_pairs = gt.shape[1]
    n_win = gt.shape[2]
    pw = 2 * hd
    nw_pad = -(-n_win // SUBLANES) * SUBLANES
    tok = pl.BlockSpec((1, tt, pw), lambda b, p: (b, 0, p))
    kern = functools.partial(_mlstm_kernel, hd=hd, n_win=n_win, n_ctx_win=n_ctx_win, tt=tt)
    return pl.pallas_call(
        kern,
        grid=(bsz, n_pairs),
        in_specs=[tok, pl.BlockSpec((1, pw, tt), lambda b, p: (b, p, 0)), tok, tok,
                  pl.BlockSpec((1, 1, n_win, GATES_PER_PAIR, MLSTM_WIN), lambda b, p: (b, p, 0, 0, 0)),
                  pl.BlockSpec((1, pw), lambda b, p: (0, p))],
        out_specs=tok,
        out_shape=jax.ShapeDtypeStruct((bsz, tt, dm), BF16),
        scratch_shapes=[pltpu.VMEM((tt, pw), F32), pltpu.VMEM((tt, pw), F32),
                        pltpu.VMEM((4, hd, 2 * hd), F32), pltpu.VMEM((4, nw_pad, MLSTM_WIN), F32),
                        pltpu.VMEM((4, nw_pad, MLSTM_WIN), F32), pltpu.VMEM((4, nw_pad, MLSTM_WIN), F32),
                        pltpu.VMEM((4, nw_pad, MLSTM_WIN), F32)],
        compiler_params=_cparams(2),
        name="mlstm",
    )(q, k_t, v, og, gt, norm_g)


def _even_out_kernel(*refs, pairs, n_res, width):
    per = 5 + n_res
    dw_ref, lg_ref, lb_ref, w_ref, pg_ref, o_ref, sh_ref = refs[2 * per:]
    s = pl.program_id(0)
    dw = dw_ref[...]
    base = CONV_HALO - width // 2
    span = sh_ref.shape[2]
    tiles = []
    for sub in range(2):
        u_ref, up_ref, un_ref, mm_ref = refs[sub * per:sub * per + 4]
        res = refs[sub * per + 4:sub * per + 4 + n_res]
        gate_ref = refs[sub * per + 4 + n_res]
        dcv = u_ref.shape[1]
        z_mix = jnp.dot(mm_ref[...], w_ref[dcv:, :], preferred_element_type=F32)
        tiles.append((u_ref, up_ref, un_ref, res, gate_ref, z_mix, dcv))
    for sub, (u_ref, up_ref, un_ref, res, gate_ref, z_mix, dcv) in enumerate(tiles):
        j = pairs.tile_in_sample(s, sub)
        pv, nv = _segment_halo_valid(j, pairs.tps_in, True)
        uext = jnp.concatenate([up_ref[...].astype(F32) * pv, u_ref[...].astype(F32), un_ref[...].astype(F32) * nv],
                               axis=0)
        for r in range(SUBLANES):
            sh_ref[sub, r] = uext[base + r:base + r + span]
        acc = jnp.zeros((TM, dcv), F32)
        for t in range(width):
            lo = SUBLANES * (t // SUBLANES)
            acc = acc + dw[t:t + 1, :] * sh_ref[sub, t % SUBLANES, lo:lo + TM, :]
        mu = jnp.mean(acc, axis=-1, keepdims=True)
        cen = acc - mu
        var = jnp.mean(cen * cen, axis=-1, keepdims=True)
        y = cen * lax.rsqrt(var + EPS) * lg_ref[...] + lb_ref[...]
        conv_out = (y * jax.nn.sigmoid(y)).astype(BF16)
        z = z_mix + jnp.dot(conv_out, w_ref[:dcv, :], preferred_element_type=F32)
        x_res = jnp.where(j == 0, res[0][0], res[1][0]) if n_res == 2 else res[0][...]
        o_ref[sub * TM:(sub + 1) * TM, :] = x_res + gate_ref[0] * _rms(z, pg_ref[...])


def _even_out(u, mm, stream, gate, post_g, dw_w, ln_g, ln_b, w_out, *, n_batch):
    xs = stream if isinstance(stream, tuple) else (stream,)
    bsz, tt, dc = u.shape
    d = xs[0].shape[2]
    tps = tt // TM
    assert (bsz * tps) % 2 == 0
    pairs = _PairTiles(tps, tps, 0, n_batch)
    flat = lambda a: a.reshape(bsz * tt, a.shape[2])
    uf, mmf = flat(u), flat(mm)
    in_specs, args = [], []
    for sub in range(2):
        in_specs += [pairs.tile(sub, dc), pairs.halo(sub, dc, CONV_HALO, False), pairs.halo(sub, dc, CONV_HALO, True),
                     pairs.tile(sub, mm.shape[2])]
        args += [uf, uf, uf, mmf]
        if len(xs) == 2:
            def ctx_index(s, sub=sub):
                return (pairs.bj(s, sub)[0], 0, 0)

            def lat_index(s, sub=sub):
                b, j = pairs.bj(s, sub)
                return (b, jnp.maximum(j - 1, 0), 0)
            in_specs += [pl.BlockSpec((1, TM, d), ctx_index), pl.BlockSpec((1, TM, d), lat_index)]
            args += list(xs)
        else:
            in_specs.append(pairs.tile(sub, d))
            args.append(flat(xs[0]))
        in_specs.append(pairs.mod(sub, d))
        args.append(gate)
    in_specs += [_const_spec(dw_w.shape), _const_spec((1, dc)), _const_spec((1, dc)), _const_spec(w_out.shape),
                 _const_spec((1, d))]
    args += [dw_w, ln_g, ln_b, w_out, post_g]
    kern = functools.partial(_even_out_kernel, pairs=pairs, n_res=len(xs), width=dw_w.shape[0])
    out = pl.pallas_call(
        kern,
        grid=(bsz * tps // 2,),
        in_specs=in_specs,
        out_specs=pl.BlockSpec((2 * TM, d), lambda s: (s, 0)),
        out_shape=jax.ShapeDtypeStruct((bsz * tt, d), F32),
        scratch_shapes=[pltpu.VMEM((2, SUBLANES, TM + SUBLANES * ((dw_w.shape[0] - 1) // SUBLANES), dc), F32)],
        compiler_params=_cparams(1),
        name="even_out",
    )(*args)
    return out.reshape(bsz, tt, d)


def _odd_proj_kernel(x_ref, sh_ref, sc_ref, g_ref, wp_ref, wqk_ref, wv_ref, cos_ref, sin_ref, qg_ref, kg_ref,
                     pool_ref, qt_ref, k_ref, vt_ref, *, hd, n_q, n_kv, q_scale):
    x = x_ref[0]
    h = (_rms(x, g_ref[...]) * (1.0 + sc_ref[0]) + sh_ref[0]).astype(BF16)
    nt_dims = (((1,), (1,)), ((), ()))
    n_heads = n_q + n_kv
    split = n_heads // 2
    pool = jnp.dot(h, wp_ref[...], preferred_element_type=F32)
    qk_parts = [lax.dot_general(wqk_ref[lo * hd:hi * hd, :], h, nt_dims, preferred_element_type=F32)
                for lo, hi in ((0, split), (split, n_heads))]
    v_t = lax.dot_general(wv_ref[...], h, nt_dims, preferred_element_type=F32)
    pool_ref[0] = pool
    cos = cos_ref[...]
    sin = sin_ref[...]
    half = hd // 2
    for hi in range(n_heads):
        part, base = (qk_parts[0], 0) if hi < split else (qk_parts[1], split)
        t = part[(hi - base) * hd:(hi - base + 1) * hd]
        gain = qg_ref[...] if hi < n_q else kg_ref[...]
        tn = t * lax.rsqrt(jnp.mean(t * t, axis=0, keepdims=True) + EPS) * gain
        x1 = tn[:half]
        x2 = tn[half:]
        rot = jnp.concatenate([x1 * cos - x2 * sin, x1 * sin + x2 * cos], axis=0)
        if hi < n_q:
            kvh, gq = divmod(hi, Q_PER_KV)
            qt_ref[0, kvh, 0, :, gq * TM:(gq + 1) * TM] = (rot * q_scale).astype(BF16)
        else:
            k_ref[0, hi - n_q] = rot.T.astype(BF16)
    for kvh in range(n_kv):
        vt_ref[0, kvh] = v_t[kvh * hd:(kvh + 1) * hd].astype(BF16)


def _odd_proj(xx, shift, scale, pre_g, w_pool, w_qk_t, w_v_t, cos_t, sin_t, qg_col, kg_col, *, n_batch, hd, n_q, n_kv):
    bsz, tt, d = xx.shape
    nt = tt // TM
    dp = w_pool.shape[1]
    kern = functools.partial(_odd_proj_kernel, hd=hd, n_q=n_q, n_kv=n_kv, q_scale=float(hd) ** -0.5 * LOG2_E)
    return pl.pallas_call(
        kern,
        grid=(bsz, nt),
        in_specs=[pl.BlockSpec((1, TM, d), lambda b, i: (b, i, 0)),
                  _mod_spec(d, 0, n_batch), _mod_spec(d, 0, n_batch), _const_spec((1, d)),
                  _const_spec(w_pool.shape), _const_spec(w_qk_t.shape), _const_spec(w_v_t.shape),
                  pl.BlockSpec((hd // 2, TM), lambda b, i: (0, i)), pl.BlockSpec((hd // 2, TM), lambda b, i: (0, i)),
                  _const_spec((hd, 1)), _const_spec((hd, 1))],
        out_specs=[pl.BlockSpec((1, TM, dp), lambda b, i: (b, i, 0)),
                   pl.BlockSpec((1, n_kv, 1, hd, Q_PER_KV * TM), lambda b, i: (b, 0, i, 0, 0)),
                   pl.BlockSpec((1, n_kv, TM, hd), lambda b, i: (b, 0, i, 0)),
                   pl.BlockSpec((1, n_kv, hd, TM), lambda b, i: (b, 0, 0, i))],
        out_shape=[jax.ShapeDtypeStruct((bsz, tt, dp), F32),
                   jax.ShapeDtypeStruct((bsz, n_kv, nt, hd, Q_PER_KV * TM), BF16),
                   jax.ShapeDtypeStruct((bsz, n_kv, tt, hd), BF16),
                   jax.ShapeDtypeStruct((bsz, n_kv, hd, tt), BF16)],
        compiler_params=_cparams(2),
        name="odd_proj",
    )(xx, shift, scale, pre_g, w_pool, w_qk_t, w_v_t, cos_t, sin_t, qg_col, kg_col)


def _attn_kernel(*refs, hd, n_q_tiles):
    qt_refs, (k_ref, vt_ref, o_ref) = refs[:n_q_tiles], refs[n_q_tiles:]
    n_kv, n_keys = k_ref.shape[1], k_ref.shape[2]
    heads = [(t, kvh, g) for t in range(n_q_tiles) for kvh in range(n_kv) for g in range(Q_PER_KV)]
    kc = min(ATTN_KEY_CHUNK, n_keys)
    n_kc = n_keys // kc

    def fold8(a, op):
        return op(a.reshape(kc // SUBLANES, SUBLANES, TM), axis=0)

    def scores(t, kvh, g):
        q_t = qt_refs[t][0, kvh, 0, :, g * TM:(g + 1) * TM]
        chunks, mx = [], None
        for j in range(n_kc):
            s = jnp.dot(k_ref[0, kvh, j * kc:(j + 1) * kc, :], q_t, preferred_element_type=F32)
            chunks.append(s)
            m8 = fold8(s, jnp.max)
            mx = m8 if mx is None else jnp.maximum(mx, m8)
        return chunks, jnp.max(mx, axis=0, keepdims=True)

    nxt = scores(*heads[0])
    for n, (t, kvh, g) in enumerate(heads):
        chunks, mx = nxt
        if n + 1 < len(heads):
            nxt = scores(*heads[n + 1])
        den8, probs = None, []
        for s in chunks:
            p = jnp.exp2(s - mx)
            d8 = fold8(p, jnp.sum)
            den8 = d8 if den8 is None else den8 + d8
            probs.append(p.astype(BF16))
        den = jnp.sum(den8, axis=0, keepdims=True)
        o_t = jnp.dot(vt_ref[0, kvh], jnp.concatenate(probs, axis=0), preferred_element_type=F32)
        col = (kvh * Q_PER_KV + g) * hd
        o_ref[0, t * TM:(t + 1) * TM, col:col + hd] = (o_t / den).T.astype(BF16)


def _attention(q_t, k, v_t, *, first_tile, n_q_tiles, n_keys):
    bsz, n_kv, nt, hd, qw = q_t.shape
    n_steps = (nt - first_tile) // n_q_tiles if n_keys == k.shape[2] else 1
    assert n_keys % min(ATTN_KEY_CHUNK, n_keys) == 0
    q_spec = lambda t: pl.BlockSpec((1, n_kv, 1, hd, qw), lambda b, i: (b, 0, first_tile + i * n_q_tiles + t, 0, 0))
    kern = functools.partial(_attn_kernel, hd=hd, n_q_tiles=n_q_tiles)
    return pl.pallas_call(
        kern,
        grid=(bsz, n_steps),
        in_specs=[q_spec(t) for t in range(n_q_tiles)]
        + [pl.BlockSpec((1, n_kv, n_keys, hd), lambda b, i: (b, 0, 0, 0)),
           pl.BlockSpec((1, n_kv, hd, n_keys), lambda b, i: (b, 0, 0, 0))],
        out_specs=pl.BlockSpec((1, n_q_tiles * TM, n_kv * Q_PER_KV * hd), lambda b, i: (b, i, 0)),
        out_shape=jax.ShapeDtypeStruct((bsz, n_steps * n_q_tiles * TM, n_kv * Q_PER_KV * hd), BF16),
        compiler_params=_cparams(2),
        name="gqa_attention",
    )(*([q_t] * n_q_tiles), k, v_t)


def _odd_out_kernel(*refs, pairs, n_attn, pool_group, seq_latent):
    per = 5 + n_attn
    pw_ref, ps_ref, w_ref, pg_ref, o_ref = refs[2 * per:]
    s = pl.program_id(0)
    tiles = []
    for sub in range(2):
        u_ref, up_ref, un_ref = refs[sub * per:sub * per + 3]
        att = refs[sub * per + 3:sub * per + 3 + n_attn]
        x_ref, gate_ref = refs[sub * per + 3 + n_attn:sub * per + 5 + n_attn]
        j = pairs.tile_in_sample(s, sub)
        dpl = u_ref.shape[1]
        attn = jnp.where(j == 0, att[1][0], att[0][...]) if n_attn == 2 else att[0][...]
        z_attn = jnp.dot(attn, w_ref[dpl:, :], preferred_element_type=F32)
        tiles.append((u_ref, up_ref, un_ref, x_ref, gate_ref, j, z_attn, dpl))
    for sub, (u_ref, up_ref, un_ref, x_ref, gate_ref, j, z_attn, dpl) in enumerate(tiles):
        pv, nv = _segment_halo_valid(j, pairs.tps_in, True)
        u = u_ref[...]
        uext = jnp.concatenate([up_ref[...] * pv, u, un_ref[...] * nv], axis=0)
        n_ext = uext.shape[0]
        s2 = uext[:n_ext - 1] + uext[1:]
        s4 = s2[:n_ext - 3] + s2[2:]
        s8 = s4[:n_ext - 7] + s4[4:]
        s16 = s8[:n_ext - 15] + s8[8:]
        sums = (s2, s4, s8, s16)
        seg_start = jnp.where(j == 0, 0, 1)
        seg_len = jnp.where(j == 0, TM, seq_latent).astype(F32)
        t = ((j - seg_start) * TM + lax.broadcasted_iota(jnp.int32, (TM, 1), 0)).astype(F32)
        lane_group = lax.broadcasted_iota(jnp.int32, (1, dpl), 1) // pool_group
        mean = jnp.zeros_like(u)
        for gi, w in enumerate(POOL_WINDOWS):
            hw = w // 2
            win = sums[gi][SMALL_HALO - hw:SMALL_HALO - hw + TM]
            cnt = jnp.minimum(t + hw, seg_len) - jnp.maximum(t - hw, 0.0)
            mean = jnp.where(lane_group == gi, win / cnt, mean)
        dpool = (mean - u).astype(BF16)
        y = jnp.dot(dpool, pw_ref[...], preferred_element_type=F32) * ps_ref[...]
        z = z_attn + jnp.dot(y.astype(BF16), w_ref[:dpl, :], preferred_element_type=F32)
        o_ref[sub * TM:(sub + 1) * TM, :] = x_ref[...] + gate_ref[0] * _rms(z, pg_ref[...])


def _odd_out(pool_u, attn, attn_ctx, xx, gate, post_g, pool_bd, pool_scale, w_out, *, n_batch, tile_off, seq_latent):
    bsz, tt, d = xx.shape
    tps = tt // TM
    tps_out = tps - tile_off
    assert (bsz * tps_out) % 2 == 0
    dp = pool_u.shape[-1]
    da = attn.shape[-1]
    pairs = _PairTiles(tps, tps_out, tile_off, n_batch)
    lat = _PairTiles(attn.shape[1] // TM, tps_out, tile_off, n_batch)
    flat = lambda a: a.reshape(a.shape[0] * a.shape[1], a.shape[2])
    uf, xf, af = flat(pool_u), flat(xx), flat(attn)
    in_specs, args = [], []
    for sub in range(2):
        def attn_index(s, sub=sub):
            b, j = lat.bj(s, sub)
            return (b * lat.tps_in + jnp.maximum(j - 1, 0), 0)
        in_specs += [pairs.tile(sub, dp), pairs.halo(sub, dp, SMALL_HALO, False), pairs.halo(sub, dp, SMALL_HALO, True),
                     pl.BlockSpec((TM, da), attn_index)]
        args += [uf, uf, uf, af]
        if tile_off == 0:
            in_specs.append(pl.BlockSpec((1, TM, da), lambda s, sub=sub: (pairs.bj(s, sub)[0], 0, 0)))
            args.append(attn_ctx)
        in_specs += [pairs.tile(sub, d), pairs.mod(sub, d)]
        args += [xf, gate]
    in_specs += [_const_spec(pool_bd.shape), _const_spec((1, dp)), _const_spec(w_out.shape), _const_spec((1, d))]
    args += [pool_bd, pool_scale, w_out, post_g]
    kern = functools.partial(_odd_out_kernel, pairs=pairs, n_attn=2 if tile_off == 0 else 1,
                             pool_group=dp // len(POOL_WINDOWS), seq_latent=seq_latent)
    out = pl.pallas_call(
        kern,
        grid=(bsz * tps_out // 2,),
        in_specs=in_specs,
        out_specs=pl.BlockSpec((2 * TM, d), lambda s: (s, 0)),
        out_shape=jax.ShapeDtypeStruct((bsz * tps_out * TM, d), F32),
        compiler_params=_cparams(1),
        name="odd_out",
    )(*args)
    return out.reshape(bsz, tps_out * TM, d)


def _ffn_kernel(*refs, tiles_per_sample, has_ctx):
    x_ref, xp_ref, xn_ref = refs[:3]
    mod_refs = refs[3:3 + 3 * FF_TILES]
    g_ref, pg_ref, wi_ref, cw_ref, wo_ref, o_ref = refs[3 + 3 * FF_TILES:]
    d_ff = wo_ref.shape[0]
    n_chunks = d_ff // FF_CHUNK
    s = pl.program_id(0)
    xs = [x_ref[sub * TM:(sub + 1) * TM] for sub in range(FF_TILES)]
    row = lax.broadcasted_iota(jnp.int32, (TM + 2 * SMALL_HALO, 1), 0)

    def normed(sub):
        j = lax.rem(FF_TILES * s + sub, tiles_per_sample)
        pv, nv = _segment_halo_valid(j, tiles_per_sample, has_ctx)
        before = xp_ref[...] if sub == 0 else xs[sub - 1][TM - SMALL_HALO:]
        after = xn_ref[...] if sub == FF_TILES - 1 else xs[sub + 1][:SMALL_HALO]
        xe = jnp.concatenate([before, xs[sub], after], axis=0)
        he = _rms(xe, g_ref[...]) * (1.0 + mod_refs[3 * sub + 1][0]) + mod_refs[3 * sub][0]
        keep = jnp.where(row < SMALL_HALO, pv, jnp.where(row >= SMALL_HALO + TM, nv, 1.0))
        return (he * keep).astype(BF16)

    def up_proj(he, c):
        lo = c * FF_CHUNK
        return (jnp.dot(he, wi_ref[:, lo:lo + FF_CHUNK], preferred_element_type=F32),
                jnp.dot(he[SMALL_HALO:SMALL_HALO + TM], wi_ref[:, d_ff + lo:d_ff + lo + FF_CHUNK],
                        preferred_element_type=F32))

    hes = [None] * FF_TILES
    items = [(sub, c) for sub in range(FF_TILES) for c in range(n_chunks)]

    def issue(n):
        nsub, nc = items[n]
        if hes[nsub] is None:
            hes[nsub] = normed(nsub)
        return up_proj(hes[nsub], nc)

    ahead = [issue(n) for n in range(FF_LOOKAHEAD)]
    acc = None
    for n, (sub, c) in enumerate(items):
        g, v = ahead.pop(0)
        if n + FF_LOOKAHEAD < len(items):
            ahead.append(issue(n + FF_LOOKAHEAD))
        cw = cw_ref[:, c * FF_CHUNK:(c + 1) * FF_CHUNK]
        gc = (cw[0:1] * g[SMALL_HALO - 1:SMALL_HALO - 1 + TM] + cw[1:2] * g[SMALL_HALO:SMALL_HALO + TM]
              + cw[2:3] * g[SMALL_HALO + 1:SMALL_HALO + 1 + TM])
        u = (gc * jax.nn.sigmoid(gc) * v).astype(BF16)
        part = jnp.dot(u, wo_ref[c * FF_CHUNK:(c + 1) * FF_CHUNK, :], preferred_element_type=F32)
        acc = part if c == 0 else acc + part
        if c == n_chunks - 1:
            o_ref[sub * TM:(sub + 1) * TM, :] = xs[sub] + mod_refs[3 * sub + 2][0] * _rms(acc, pg_ref[...])


def _ffn(xx, shift, scale, gate, pre_g, post_g, w_in, conv_w, w_out, *, n_batch, has_ctx):
    bsz, tt, d = xx.shape
    tps = tt // TM
    n_tiles = bsz * tps
    assert n_tiles % FF_TILES == 0
    hb = TM // SMALL_HALO
    n_hblk = bsz * tt // SMALL_HALO

    def mod_spec(sub):
        def index(s):
            g = FF_TILES * s + sub
            b = g // tps
            return (jnp.where(g % tps == 0, n_batch, b) if has_ctx else b, 0, 0)
        return pl.BlockSpec((1, 1, d), index)

    kern = functools.partial(_ffn_kernel, tiles_per_sample=tps, has_ctx=has_ctx)
    xf = xx.reshape(bsz * tt, d)
    mod_specs, mod_args = [], []
    for sub in range(FF_TILES):
        mod_specs += [mod_spec(sub)] * 3
        mod_args += [shift, scale, gate]
    out = pl.pallas_call(
        kern,
        grid=(n_tiles // FF_TILES,),
        in_specs=[pl.BlockSpec((FF_TILES * TM, d), lambda s: (s, 0)),
                  pl.BlockSpec((SMALL_HALO, d), lambda s: (jnp.maximum(FF_TILES * s * hb - 1, 0), 0)),
                  pl.BlockSpec((SMALL_HALO, d), lambda s: (jnp.minimum(FF_TILES * (s + 1) * hb, n_hblk - 1), 0))]
        + mod_specs
        + [_const_spec((1, d)), _const_spec((1, d)),
           _const_spec(w_in.shape), _const_spec(conv_w.shape), _const_spec(w_out.shape)],
        out_specs=pl.BlockSpec((FF_TILES * TM, d), lambda s: (s, 0)),
        out_shape=jax.ShapeDtypeStruct((bsz * tt, d), F32),
        compiler_params=_cparams(1),
        name="conv_ffn",
    )(xf, xf, xf, *mod_args, pre_g, post_g, w_in, conv_w, w_out)
    return out.reshape(bsz, tt, d)


def _rope_tables_t(seq, ctx_len, hd):
    rows = seq // GRID_W
    row = jnp.repeat(jnp.arange(rows), GRID_W).astype(F32)
    col = jnp.tile(jnp.arange(GRID_W), rows).astype(F32)
    n_freq = hd // 4
    inv = ROPE_THETA ** (-jnp.arange(n_freq, dtype=F32) / n_freq)
    ang = jnp.concatenate([row[:, None] * inv, col[:, None] * inv], axis=-1)
    cos = jnp.concatenate([jnp.ones((ctx_len, hd // 2), F32), jnp.cos(ang)], axis=0)
    sin = jnp.concatenate([jnp.zeros((ctx_len, hd // 2), F32), jnp.sin(ang)], axis=0)
    return cos.T, sin.T


def kernel(x, c, ctx, c_ctx, ada_w, ada_b, mix_pre_g, mix_post_g, ffn_pre_g, ffn_post_g, ffn_w_in, ffn_conv_w,
           ffn_w_out, even_w_in, even_w_out, conv_dw_w, conv_ln_g, conv_ln_b, mlstm_gate_b, mlstm_norm_g,
           odd_w_in, odd_w_out, pool_w, pool_scale, q_norm_g, k_norm_g):
    bsz, seq, d = x.shape
    ctx_len = ctx.shape[1]
    depth = ada_w.shape[0]
    assert ctx_len == TM and seq % TM == 0 and seq % GRID_W == 0 and depth >= 1
    dc = conv_dw_w.shape[2]
    dm = mlstm_norm_g.shape[1]
    n_heads = mlstm_gate_b.shape[2]
    hd_m = dm // n_heads
    assert n_heads % 2 == 0 and even_w_in.shape[2] == 2 * dc + 4 * dm + 4 * n_heads
    hd = q_norm_g.shape[1]
    dp = pool_scale.shape[1]
    n_kv = (odd_w_in.shape[2] - dp - (odd_w_out.shape[1] - dp)) // (2 * hd)
    n_q = (odd_w_out.shape[1] - dp) // hd
    assert n_q == n_kv * Q_PER_KV
    assert ffn_w_out.shape[1] % FF_CHUNK == 0

    pad = (-(bsz + 1)) % SUBLANES
    cc = jnp.concatenate([c, c_ctx[None, :], jnp.zeros((pad, d), F32)], axis=0)
    mods = _ada_all(cc, ada_w, ada_b)

    def mod(l, k):
        return mods[l, :bsz + 1, k * d:(k + 1) * d].reshape(bsz + 1, 1, d)

    cos_t, sin_t = _rope_tables_t(seq, ctx_len, hd)
    perm = np.concatenate([np.arange(0, hd, 2), np.arange(1, hd, 2)])

    xx = (ctx, x)
    row2 = lambda a: a.reshape(1, -1)

    for l in range(depth):
        last = l == depth - 1
        if l % 2 == 0:
            e = l // 2
            w_in = even_w_in[e]
            n_main = 2 * dc + 4 * dm
            w_main = w_in[:, :n_main].astype(BF16)
            wg = w_in[:, n_main:].reshape(d, 4, n_heads // 2, 2)
            w_gate_t = wg.transpose(2, 1, 3, 0).reshape(4 * n_heads, d).astype(BF16)
            gate_b_col = mlstm_gate_b[e].reshape(4, n_heads // 2, 2).transpose(1, 0, 2).reshape(4 * n_heads, 1)
            w_k_t = w_in[:, 2 * dc + dm:2 * dc + 2 * dm].T.astype(BF16)
            u, q, k_t, v, og, gt = _even_proj(xx, mod(l, 0), mod(l, 1), row2(mix_pre_g[l]), w_main, w_k_t, w_gate_t,
                                              gate_b_col, n_batch=bsz, dc=dc, dm=dm, hd=hd_m)
            mm = _mlstm(q, k_t, v, og, gt, row2(mlstm_norm_g[e]), hd=hd_m, n_ctx_win=ctx_len // MLSTM_WIN)
            xx = _even_out(u, mm, xx, mod(l, 2), row2(mix_post_g[l]), conv_dw_w[e], row2(conv_ln_g[e]),
                           row2(conv_ln_b[e]), even_w_out[e].astype(BF16), n_batch=bsz)
        else:
            o = l // 2
            w_in = odd_w_in[o]
            w_pool = w_in[:, :dp].astype(BF16)
            w_qk = w_in[:, dp:dp + (n_q + n_kv) * hd].reshape(d, n_q + n_kv, hd)[:, :, perm]
            w_qk_t = w_qk.reshape(d, -1).T.astype(BF16)
            w_v_t = w_in[:, dp + (n_q + n_kv) * hd:].T.astype(BF16)
            pool_u, q_t, k, v_t = _odd_proj(xx, mod(l, 0), mod(l, 1), row2(mix_pre_g[l]), w_pool, w_qk_t, w_v_t,
                                            cos_t, sin_t, q_norm_g[o][perm].reshape(hd, 1),
                                            k_norm_g[o][perm].reshape(hd, 1), n_batch=bsz, hd=hd, n_q=n_q, n_kv=n_kv)
            tile_off = 1 if last else 0
            attn = _attention(q_t, k, v_t, first_tile=1, n_q_tiles=2, n_keys=k.shape[2])
            attn_ctx = None if last else _attention(q_t, k, v_t, first_tile=0, n_q_tiles=1, n_keys=ctx_len)
            pg = dp // len(POOL_WINDOWS)
            pool_bd = jnp.zeros((dp, dp), F32)
            for gi in range(len(POOL_WINDOWS)):
                pool_bd = pool_bd.at[gi * pg:(gi + 1) * pg, gi * pg:(gi + 1) * pg].set(pool_w[o, gi])
            xx = _odd_out(pool_u, attn, attn_ctx, xx, mod(l, 2), row2(mix_post_g[l]), pool_bd.astype(BF16),
                          row2(pool_scale[o]), odd_w_out[o].astype(BF16), n_batch=bsz, tile_off=tile_off,
                          seq_latent=seq)
        xx = _ffn(xx, mod(l, 3), mod(l, 4), mod(l, 5), row2(ffn_pre_g[l]), row2(ffn_post_g[l]),
                  ffn_w_in[l].astype(BF16), ffn_conv_w[l], ffn_w_out[l].astype(BF16), n_batch=bsz,
                  has_ctx=xx.shape[1] != seq)
    return xx if xx.shape[1] == seq else xx[:, ctx_len:]
```
